```python
import math
import jax, jax.numpy as jnp
from jax import lax
import numpy as np

D_MODEL = 2048
BATCH = 4
SEQ = 2048
DEPTH = 1
DEC_BATCH = 128
DEC_SEQ = 8
PAST_LEN = 16384
PAGE_SIZE = 128

SSD_WIDTH = D_MODEL
SSD_HEAD_DIM = 64
SSD_HEADS = SSD_WIDTH // SSD_HEAD_DIM
SSD_GROUPS = 4
SSD_HEADS_PER_GROUP = SSD_HEADS // SSD_GROUPS
SSD_STATE = 128
SSD_CONV = 4
SSD_CONV_DIM = SSD_WIDTH + 2 * SSD_GROUPS * SSD_STATE
SSD_CHUNK = 128
SC_WIDTH = D_MODEL
SC_CONV = 3
MIX_WIDTH = SSD_WIDTH + SC_WIDTH
IN_DIM = SSD_WIDTH + SSD_CONV_DIM + SSD_HEADS + 3 * SC_WIDTH
MEM_TOKENS = 256
MEM_HEADS = 4
MEM_HEAD_DIM = D_MODEL // MEM_HEADS
PEER_HEADS = 8
PEER_NKEYS = 128
PEER_EXPERTS = PEER_NKEYS * PEER_NKEYS
PEER_QDIM = 256
PEER_HALF = PEER_QDIM // 2
PEER_TOPK = 16
PEER_BLOCK = 64
EPS = 1e-6

kernel_name = 'hymba_ssd_shortconv_peer_memxattn_step'


def rmsnorm(x, g):
    xf = x.astype(jnp.float32)
    y = xf * lax.rsqrt(jnp.mean(xf * xf, axis=-1, keepdims=True) + EPS)
    return (y * g.astype(jnp.float32)).astype(x.dtype)


def causal_dwconv(x, buf, w, b=None):
    k = w.shape[0]
    n = x.shape[1]
    xp = jnp.concatenate([buf.astype(x.dtype), x], axis=1)
    y = xp[:, 0:n] * w[0]
    for j in range(1, k):
        y = y + xp[:, j:j + n] * w[j]
    if b is not None:
        y = y + b
    return y, xp[:, n:]


def ssd_scan(x, dt, a_log, bm, cm, d_skip, init_state):
    f32 = jnp.float32
    G, R, P, N = SSD_GROUPS, SSD_HEADS_PER_GROUP, SSD_HEAD_DIM, SSD_STATE
    b, L = x.shape[0], x.shape[1]
    cs = min(SSD_CHUNK, L)
    nc = -(-L // cs)
    pad = nc * cs - L
    xf, dtf, bf, cf = x.astype(f32), dt.astype(f32), bm.astype(f32), cm.astype(f32)
    if pad:
        padt = lambda t: jnp.pad(t, [(0, 0), (0, pad)] + [(0, 0)] * (t.ndim - 2))
        xf, dtf, bf, cf = padt(xf), padt(dtf), padt(bf), padt(cf)
    A = -jnp.exp(a_log.astype(f32)).reshape(G, R)
    a = (dtf * A).reshape(b, nc, cs, G, R)
    xdt = (xf * dtf[..., None]).reshape(b, nc, cs, G, R, P)
    bc = bf.reshape(b, nc, cs, G, N)
    cc = cf.reshape(b, nc, cs, G, N)
    a_cs = jnp.cumsum(jnp.transpose(a, (0, 3, 4, 1, 2)), axis=-1)
    tril = jnp.tril(jnp.ones((cs, cs), dtype=bool))
    lmat = jnp.exp(jnp.where(tril, a_cs[..., :, None] - a_cs[..., None, :], -jnp.inf))
    cb = jnp.einsum('bclgn,bcsgn->bgcls', cc, bc)
    y_diag = jnp.einsum('bgcls,bgrcls,bcsgrp->bclgrp', cb, lmat, xdt)
    decay_states = jnp.exp(a_cs[..., -1:] - a_cs)
    states = jnp.einsum('bcsgn,bgrcs,bcsgrp->bcgrpn', bc, decay_states, xdt)
    states = jnp.concatenate([init_state.astype(f32)[:, None], states], axis=1)
    tot = jnp.cumsum(jnp.pad(a_cs[..., -1], [(0, 0), (0, 0), (0, 0), (1, 0)]), axis=-1)
    tril2 = jnp.tril(jnp.ones((nc + 1, nc + 1), dtype=bool))
    decay_chunk = jnp.exp(jnp.where(tril2, tot[..., :, None] - tot[..., None, :], -jnp.inf))
    new_states = jnp.einsum('bgrzc,bcgrpn->bzgrpn', decay_chunk, states)
    y_off = jnp.einsum('bclgn,bcgrpn,bgrcl->bclgrp', cc, new_states[:, :-1], jnp.exp(a_cs))
    y = (y_diag + y_off).reshape(b, nc * cs, G, R, P)[:, :L]
    y = y + d_skip.astype(f32).reshape(G, R)[:, :, None] * x.astype(f32)
    return y, new_states[:, -1]


def mixer(xn, ssm_state, ssd_buf, sc_buf, w_in, conv_w, conv_b, dt_bias, a_log, d_skip, ssd_norm, sc_w, w_out):
    f32 = jnp.float32
    b, L, _ = xn.shape
    G, R, P, N = SSD_GROUPS, SSD_HEADS_PER_GROUP, SSD_HEAD_DIM, SSD_STATE
    s1 = SSD_WIDTH
    s2 = s1 + SSD_CONV_DIM
    s3 = s2 + SSD_HEADS
    s4 = s3 + SC_WIDTH
    s5 = s4 + SC_WIDTH
    h = xn @ w_in
    z, xbc, dt, sc_h, sc_b, sc_c = jnp.split(h, [s1, s2, s3, s4, s5], axis=-1)
    xbc, new_ssd_buf = causal_dwconv(xbc, ssd_buf, conv_w, conv_b)
    xbc = jax.nn.silu(xbc)
    xs, bm, cm = jnp.split(xbc, [SSD_WIDTH, SSD_WIDTH + G * N], axis=-1)
    dt = jax.nn.softplus(dt.astype(f32) + dt_bias.astype(f32))
    y, new_state = ssd_scan(xs.reshape(b, L, G, R, P), dt.reshape(b, L, G, R), a_log,
                            bm.reshape(b, L, G, N), cm.reshape(b, L, G, N), d_skip,
                            ssm_state.reshape(b, G, R, P, N))
    yg = (y.reshape(b, L, SSD_WIDTH) * jax.nn.silu(z.astype(f32))).reshape(b, L, G, SSD_WIDTH // G)
    yg = yg * lax.rsqrt(jnp.mean(yg * yg, axis=-1, keepdims=True) + EPS)
    y_ssd = (yg.reshape(b, L, SSD_WIDTH) * ssd_norm.astype(f32)).astype(xn.dtype)
    cv, new_sc_buf = causal_dwconv(sc_c * sc_h, sc_buf, sc_w)
    y_sc = sc_b * cv
    out = jnp.concatenate([y_ssd, y_sc], axis=-1) @ w_out
    return out, new_state.reshape(b, SSD_HEADS, P, N), new_ssd_buf, new_sc_buf


def mem_kv(mem, g, wk, wv):
    b = mem.shape[0]
    mn = rmsnorm(mem, g)
    k = (mn @ wk).reshape(b, MEM_TOKENS, MEM_HEADS, MEM_HEAD_DIM)
    v = (mn @ wv).reshape(b, MEM_TOKENS, MEM_HEADS, MEM_HEAD_DIM)
    return k, v


def mem_attend(xn, k, v, wq, wo):
    b, L, _ = xn.shape
    q = (xn @ wq).reshape(b, L, MEM_HEADS, MEM_HEAD_DIM)
    s = jnp.einsum('blhd,bmhd->bhlm', q, k.astype(q.dtype)).astype(jnp.float32) * (1.0 / math.sqrt(MEM_HEAD_DIM))
    p = jax.nn.softmax(s, axis=-1)
    o = jnp.einsum('bhlm,bmhd->blhd', p.astype(v.dtype), v).reshape(b, L, D_MODEL)
    return (o @ wo).astype(xn.dtype)


def peer(xn, wq, sub_keys, u, v):
    f32 = jnp.float32
    b, L, D = xn.shape
    T = b * L
    K = PEER_TOPK
    xt = xn.reshape(T, D)
    q = (xt @ wq).reshape(T, PEER_HEADS, 2, PEER_HALF)
    s = jnp.einsum('thid,hikd->thik', q, sub_keys).astype(f32)
    s1, i1 = lax.top_k(s[:, :, 0], K)
    s2, i2 = lax.top_k(s[:, :, 1], K)
    comb = (s1[..., :, None] + s2[..., None, :]).reshape(T, PEER_HEADS, K * K)
    cid = (i1[..., :, None] * PEER_NKEYS + i2[..., None, :]).reshape(T, PEER_HEADS, K * K)
    sf, pos = lax.top_k(comb, K)
    eid = jnp.take_along_axis(cid, pos, axis=-1).reshape(T, PEER_HEADS * K)
    g = jax.nn.softmax(sf, axis=-1).reshape(T, PEER_HEADS * K)
    nb = -(-T // PEER_BLOCK)
    pad = nb * PEER_BLOCK - T
    xb = jnp.pad(xt, ((0, pad), (0, 0))).reshape(nb, PEER_BLOCK, D)
    eb = jnp.pad(eid, ((0, pad), (0, 0))).reshape(nb, PEER_BLOCK, PEER_HEADS * K)
    gb = jnp.pad(g, ((0, pad), (0, 0))).reshape(nb, PEER_BLOCK, PEER_HEADS * K)

    def block(args):
        xi, ei, gi = args
        act = jax.nn.gelu(jnp.einsum('td,tkd->tk', xi, u[ei]).astype(f32), approximate=False)
        return jnp.einsum('tk,tkd->td', (gi * act).astype(v.dtype), v[ei]).astype(xi.dtype)

    out = lax.map(block, (xb, eb, gb)).reshape(nb * PEER_BLOCK, D)[:T]
    return out.reshape(b, L, D)


def layer(x, mem_k, mem_v, ssm_state, ssd_buf, sc_buf, mix_p, mem_p, ffn_p):
    norm_mix, w_in, conv_w, conv_b, dt_bias, a_log, d_skip, ssd_norm, sc_w, w_out = mix_p
    norm_mem_q, w_mem_q, w_mem_o = mem_p
    norm_ffn, w_peer_q, sub_keys, peer_u, peer_v = ffn_p
    m, new_ssm, new_ssd_buf, new_sc_buf = mixer(rmsnorm(x, norm_mix), ssm_state, ssd_buf, sc_buf, w_in, conv_w,
                                                conv_b, dt_bias, a_log, d_skip, ssd_norm, sc_w, w_out)
    x = x + m
    x = x + mem_attend(rmsnorm(x, norm_mem_q), mem_k, mem_v, w_mem_q, w_mem_o)
    x = x + peer(rmsnorm(x, norm_ffn), w_peer_q, sub_keys, peer_u, peer_v)
    return x, new_ssm, new_ssd_buf, new_sc_buf


def setup_inputs(seed: int = 0) -> dict:
    key = jax.random.key(seed)
    ks = jax.random.split(key, 32)
    f32 = jnp.float32
    nrm = lambda k, shape, s: jax.random.normal(k, shape, f32) * s
    gain = lambda k, shape: 1.0 + 0.02 * jax.random.normal(k, shape, f32)
    dt0 = jnp.exp(jax.random.uniform(ks[13], (DEPTH, SSD_HEADS), f32, math.log(1e-3), math.log(1e-1)))
    return {
        'x_prompt': nrm(ks[0], (BATCH, SEQ, D_MODEL), 1.0),
        'x_sample': nrm(ks[1], (DEC_BATCH, DEC_SEQ, D_MODEL), 1.0),
        'mem_prompt': nrm(ks[2], (BATCH, MEM_TOKENS, D_MODEL), 1.0),
        'state_ssm': nrm(ks[3], (DEPTH, DEC_BATCH, SSD_HEADS, SSD_HEAD_DIM, SSD_STATE), 0.5),
        'state_ssd_conv': nrm(ks[4], (DEPTH, DEC_BATCH, SSD_CONV - 1, SSD_CONV_DIM), 1.0),
        'state_short_conv': nrm(ks[5], (DEPTH, DEC_BATCH, SC_CONV - 1, SC_WIDTH), 1.0),
        'cache_mem_k': nrm(ks[6], (DEPTH, DEC_BATCH, MEM_TOKENS, MEM_HEADS, MEM_HEAD_DIM), 1.0),
        'cache_mem_v': nrm(ks[7], (DEPTH, DEC_BATCH, MEM_TOKENS, MEM_HEADS, MEM_HEAD_DIM), 1.0),
        'norm_mix': gain(ks[8], (DEPTH, D_MODEL)),
        'w_in': nrm(ks[9], (DEPTH, D_MODEL, IN_DIM), D_MODEL ** -0.5),
        'ssd_conv_w': nrm(ks[10], (DEPTH, SSD_CONV, SSD_CONV_DIM), SSD_CONV ** -0.5),
        'ssd_conv_b': nrm(ks[11], (DEPTH, SSD_CONV_DIM), 0.02),
        'ssd_dt_bias': dt0 + jnp.log(-jnp.expm1(-dt0)),
        'ssd_a_log': jnp.log(jax.random.uniform(ks[12], (DEPTH, SSD_HEADS), f32, 1.0, 16.0)),
        'ssd_d': 1.0 + nrm(ks[14], (DEPTH, SSD_HEADS), 0.1),
        'ssd_norm': gain(ks[15], (DEPTH, SSD_WIDTH)),
        'sc_conv_w': nrm(ks[16], (DEPTH, SC_CONV, SC_WIDTH), SC_CONV ** -0.5),
        'w_out': nrm(ks[17], (DEPTH, MIX_WIDTH, D_MODEL), MIX_WIDTH ** -0.5),
        'norm_mem_q': gain(ks[18], (DEPTH, D_MODEL)),
        'norm_mem_kv': gain(ks[19], (DEPTH, D_MODEL)),
        'w_mem_q': nrm(ks[20], (DEPTH, D_MODEL, D_MODEL), D_MODEL ** -0.5),
        'w_mem_k': nrm(ks[21], (DEPTH, D_MODEL, D_MODEL), D_MODEL ** -0.5),
        'w_mem_v': nrm(ks[22], (DEPTH, D_MODEL, D_MODEL), D_MODEL ** -0.5),
        'w_mem_o': nrm(ks[23], (DEPTH, D_MODEL, D_MODEL), D_MODEL ** -0.5),
        'norm_ffn': gain(ks[24], (DEPTH, D_MODEL)),
        'w_peer_q': nrm(ks[25], (DEPTH, D_MODEL, PEER_HEADS * PEER_QDIM), D_MODEL ** -0.5),
        'peer_sub_keys': nrm(ks[26], (DEPTH, PEER_HEADS, 2, PEER_NKEYS, PEER_HALF), PEER_HALF ** -0.5),
        'peer_u': nrm(ks[27], (DEPTH, PEER_EXPERTS, D_MODEL), D_MODEL ** -0.5),
        'peer_v': nrm(ks[28], (DEPTH, PEER_EXPERTS, D_MODEL), 0.1),
        'norm_final': gain(ks[29], (D_MODEL,)),
    }


def reference(x_prompt, x_sample, mem_prompt, state_ssm, state_ssd_conv, state_short_conv, cache_mem_k,
              cache_mem_v, norm_mix, w_in, ssd_conv_w, ssd_conv_b, ssd_dt_bias, ssd_a_log, ssd_d, ssd_norm,
              sc_conv_w, w_out, norm_mem_q, norm_mem_kv, w_mem_q, w_mem_k, w_mem_v, w_mem_o, norm_ffn,
              w_peer_q, peer_sub_keys, peer_u, peer_v, norm_final):
    bp = x_prompt.shape[0]
    hp, hs = x_prompt, x_sample
    p_ssm, p_ssd_conv, p_sc, p_mk, p_mv = [], [], [], [], []
    s_ssm, s_ssd_conv, s_sc = [], [], []
    for i in range(DEPTH):
        mix_p = (norm_mix[i], w_in[i], ssd_conv_w[i], ssd_conv_b[i], ssd_dt_bias[i], ssd_a_log[i], ssd_d[i],
                 ssd_norm[i], sc_conv_w[i], w_out[i])
        mem_p = (norm_mem_q[i], w_mem_q[i], w_mem_o[i])
        ffn_p = (norm_ffn[i], w_peer_q[i], peer_sub_keys[i], peer_u[i], peer_v[i])
        mk, mv = mem_kv(mem_prompt, norm_mem_kv[i], w_mem_k[i], w_mem_v[i])
        z_ssm = jnp.zeros((bp, SSD_HEADS, SSD_HEAD_DIM, SSD_STATE), jnp.float32)
        z_ssd_buf = jnp.zeros((bp, SSD_CONV - 1, SSD_CONV_DIM), x_prompt.dtype)
        z_sc_buf = jnp.zeros((bp, SC_CONV - 1, SC_WIDTH), x_prompt.dtype)
        hp, n_ssm, n_ssd_buf, n_sc_buf = layer(hp, mk, mv, z_ssm, z_ssd_buf, z_sc_buf, mix_p, mem_p, ffn_p)
        p_ssm.append(n_ssm)
        p_ssd_conv.append(n_ssd_buf)
        p_sc.append(n_sc_buf)
        p_mk.append(mk)
        p_mv.append(mv)
        hs, m_ssm, m_ssd_buf, m_sc_buf = layer(hs, cache_mem_k[i], cache_mem_v[i], state_ssm[i],
                                               state_ssd_conv[i], state_short_conv[i], mix_p, mem_p, ffn_p)
        s_ssm.append(m_ssm)
        s_ssd_conv.append(m_ssd_buf)
        s_sc.append(m_sc_buf)
    y_prompt = rmsnorm(hp, norm_final)
    y_sample = rmsnorm(hs, norm_final)
    return (y_prompt, y_sample, jnp.stack(p_ssm), jnp.stack(p_ssd_conv), jnp.stack(p_sc), jnp.stack(p_mk),
            jnp.stack(p_mv), jnp.stack(s_ssm), jnp.stack(s_ssd_conv), jnp.stack(s_sc))
```

```python
import functools
import math

import jax
import jax.numpy as jnp
import numpy as np
from jax import lax
from jax.experimental import pallas as pl
from jax.experimental.pallas import tpu as pltpu

F32 = jnp.float32
BF16 = jnp.bfloat16

EPS = 1e-6
SSD_CHUNK = 128
PEER_TOPK = 16
LANES = 128
SUBLANES = 8
VMEM_LIMIT_BYTES = 56 * 1024 * 1024
NEG_INF = float("-inf")


def _cparams(sem):
    return pltpu.CompilerParams(dimension_semantics=sem, vmem_limit_bytes=VMEM_LIMIT_BYTES)


def _tile(n, pref, mult):
    if n <= pref:
        return n
    t = (pref // mult) * mult
    while t >= mult:
        if n % t == 0:
            return t
        t -= mult
    return n


def _hdot(a, b):
    return jnp.dot(a, b, precision=lax.Precision.HIGHEST, preferred_element_type=F32)


def _hdot_nt(a, b):
    return lax.dot_general(a, b, (((1,), (1,)), ((), ())), precision=lax.Precision.HIGHEST,
                           preferred_element_type=F32)


def _dot(a, b):
    return jnp.dot(a, b, preferred_element_type=F32)


def _dot_nt(a, b):
    return lax.dot_general(a, b, (((1,), (1,)), ((), ())), preferred_element_type=F32)


def _silu(x):
    return x * (1.0 / (1.0 + jnp.exp(-x)))


def _softplus(x):
    return jnp.maximum(x, 0.0) + jnp.log1p(jnp.exp(-jnp.abs(x)))


def _gelu_exact(x):
    return 0.5 * x * (1.0 + lax.erf(x * (1.0 / math.sqrt(2.0))))


def _rms_matmul_kernel(x_ref, g_ref, w_ref, o_ref, xn_ref):
    @pl.when(pl.program_id(1) == 0)
    def _():
        x = x_ref[...]
        ms = jnp.mean(x * x, axis=-1, keepdims=True)
        xn_ref[...] = (x * lax.rsqrt(ms + EPS) * g_ref[...]).astype(BF16)

    o_ref[...] = _dot(xn_ref[...], w_ref[...]).astype(o_ref.dtype)


def _rms_matmul(x, g, w, out_dtype=F32):
    t, d = x.shape
    n = w.shape[1]
    tm = _tile(t, 1024, SUBLANES)
    tn = _tile(n, 512, LANES)
    return pl.pallas_call(
        _rms_matmul_kernel,
        grid=(t // tm, n // tn),
        in_specs=[pl.BlockSpec((tm, d), lambda i, j: (i, 0)),
                  pl.BlockSpec((1, d), lambda i, j: (0, 0)),
                  pl.BlockSpec((d, tn), lambda i, j: (0, j))],
        out_specs=pl.BlockSpec((tm, tn), lambda i, j: (i, j)),
        out_shape=jax.ShapeDtypeStruct((t, n), out_dtype),
        scratch_shapes=[pltpu.VMEM((tm, d), BF16)],
        compiler_params=_cparams(("parallel", "arbitrary")),
        name="rms_matmul",
    )(x, g.reshape(1, d), w)


def _matmul_res_kernel(*refs, n_a):
    a_refs = refs[:n_a]
    w_refs = refs[n_a:2 * n_a]
    r_ref, o_ref = refs[2 * n_a], refs[2 * n_a + 1]
    ab_refs = refs[2 * n_a + 2:]

    @pl.when(pl.program_id(1) == 0)
    def _():
        for a_ref, ab_ref in zip(a_refs, ab_refs):
            ab_ref[...] = a_ref[...].astype(BF16)

    acc = r_ref[...]
    for ab_ref, w_ref in zip(ab_refs, w_refs):
        acc = acc + _dot(ab_ref[...], w_ref[...])
    o_ref[...] = acc


def _matmul_res(a_list, w_list, res):
    t, d = res.shape
    tm = _tile(t, 1024, SUBLANES)
    tn = _tile(d, 512, LANES)
    n_a = len(a_list)
    in_specs = ([pl.BlockSpec((tm, a.shape[1]), lambda i, j: (i, 0)) for a in a_list]
                + [pl.BlockSpec((w.shape[0], tn), lambda i, j: (0, j)) for w in w_list]
                + [pl.BlockSpec((tm, tn), lambda i, j: (i, j))])
    return pl.pallas_call(
        functools.partial(_matmul_res_kernel, n_a=n_a),
        grid=(t // tm, d // tn),
        in_specs=in_specs,
        out_specs=pl.BlockSpec((tm, tn), lambda i, j: (i, j)),
        out_shape=jax.ShapeDtypeStruct((t, d), F32),
        scratch_shapes=[pltpu.VMEM((tm, a.shape[1]), BF16) for a in a_list],
        compiler_params=_cparams(("parallel", "arbitrary")),
        name="matmul_res",
    )(*a_list, *w_list, res)


def _mixer_kernel(xs_ref, z_ref, sch_ref, scb_ref, scc_ref, bm_ref, cm_ref, dt_ref,
                  s0_ref, tx_ref, tb_ref, tc_ref, tsc_ref,
                  cwx_ref, cwb_ref, cwc_ref, cbx_ref, cbb_ref, cbc_ref,
                  dtb_ref, alog_ref, sel_ref, dx_ref, nw_ref, scw_ref,
                  e_ref, et_ref, tri_ref, blk_ref, eye_ref,
                  yssd_ref, ysc_ref, sout_ref, ntx_ref, ntb_ref, ntc_ref, ntsc_ref,
                  s_scr, padx, padb, padc, padu,
                  *, lc, nb, gb, rh, p, n, n_chunks, time_major):
    rows = lc * nb
    gw = rh * p
    hr = padx.shape[0] - rows
    k_ssd = cwx_ref.shape[0]
    k_sc = scw_ref.shape[0]
    c = pl.program_id(2)

    def ld(ref):
        if time_major:
            return ref[...].reshape(ref.shape[0] * ref.shape[1], ref.shape[2])
        return ref[0]

    def st(ref, val):
        if time_major:
            ref[...] = val.reshape(ref.shape)
        else:
            ref[0] = val

    @pl.when(c == 0)
    def _():
        s_scr[...] = s0_ref[...]
        padx[hr - (k_ssd - 1) * nb:hr, :] = ld(tx_ref)
        padb[hr - (k_ssd - 1) * nb:hr, :] = ld(tb_ref)
        padc[hr - (k_ssd - 1) * nb:hr, :] = ld(tc_ref)
        padu[hr - (k_sc - 1) * nb:hr, :] = ld(tsc_ref)

    def conv(pad, x, w_ref, nk):
        pad[hr:hr + rows, :] = x
        acc = None
        for j in range(nk):
            off = hr - (nk - 1 - j) * nb
            term = pad[off:off + rows, :] * w_ref[j:j + 1, :]
            acc = term if acc is None else acc + term
        tail = pad[hr + rows - (nk - 1) * nb:hr + rows, :]
        if n_chunks > 1:
            pad[hr - (nk - 1) * nb:hr, :] = tail
        return acc, tail

    xs_pre, tail_x = conv(padx, ld(xs_ref), cwx_ref, k_ssd)
    b_pre, tail_b = conv(padb, ld(bm_ref), cwb_ref, k_ssd)
    c_pre, tail_c = conv(padc, ld(cm_ref), cwc_ref, k_ssd)
    cv, tail_u = conv(padu, ld(scc_ref) * ld(sch_ref), scw_ref, k_sc)
    st(ntx_ref, tail_x)
    st(ntb_ref, tail_b)
    st(ntc_ref, tail_c)
    st(ntsc_ref, tail_u)
    st(ysc_ref, ld(scb_ref) * cv)

    xs_c = _silu(xs_pre + cbx_ref[...])
    b_c = _silu(b_pre + cbb_ref[...])
    c_c = _silu(c_pre + cbc_ref[...])

    dt_all = _softplus(ld(dt_ref) + dtb_ref[...])
    a_all = dt_all * (-jnp.exp(alog_ref[...]))
    sel = sel_ref[0]
    dt = _hdot(dt_all, sel)
    a = _hdot(a_all, sel)
    tri = tri_ref[...]
    a_cs = _hdot(tri, a)
    a_tot = _hdot(blk_ref[...], a)
    e = e_ref[...]
    dt_x = _hdot(dt, e)
    wst_x = _hdot(jnp.exp(a_tot - a_cs), e)
    woff_x = _hdot(jnp.exp(a_cs), e)
    xdt = xs_c * dt_x
    xw = xdt * wst_x
    eye = eye_ref[...]
    a_cs_t = _hdot_nt(eye, a_cs)
    tri_mask = tri > 0.5
    lane_id = lax.broadcasted_iota(jnp.int32, (rows, LANES), 1)
    row_seq_g = lax.broadcasted_iota(jnp.int32, (rows, gw), 0) % nb
    row_seq_n = lax.broadcasted_iota(jnp.int32, (rows, n), 0) % nb

    decs = []
    for b in range(nb):
        row_b = jnp.broadcast_to(a_tot[b:b + 1, :], (n, LANES))
        d_b = jnp.exp(_hdot_nt(eye, row_b))
        decs.append(_hdot(et_ref[...], d_b))

    z = ld(z_ref)
    y_cols = []
    for gl in range(gb):
        cg = c_c[:, gl * n:(gl + 1) * n].astype(BF16)
        bg = b_c[:, gl * n:(gl + 1) * n]
        cb = _dot_nt(cg, bg.astype(BF16))
        ydiag = []
        for pr in range(gw // LANES):
            xpair = xdt[:, gl * gw + pr * LANES:gl * gw + (pr + 1) * LANES]
            acc = None
            for sub in range(LANES // p):
                hl = gl * rh + pr * (LANES // p) + sub
                col = a_cs[:, hl:hl + 1]
                row = a_cs_t[hl:hl + 1, :]
                lmat = jnp.exp(jnp.where(tri_mask, col - row, NEG_INF))
                m = (cb * lmat).astype(BF16)
                xm = jnp.where((lane_id >= sub * p) & (lane_id < (sub + 1) * p), xpair, 0.0).astype(BF16)
                t = _dot(m, xm)
                acc = t if acc is None else acc + t
            ydiag.append(acc)
        ydiag = ydiag[0] if len(ydiag) == 1 else jnp.concatenate(ydiag, axis=1)

        xw_t = xw[:, gl * gw:(gl + 1) * gw].T.astype(BF16)
        yoff = None
        for b in range(nb):
            s_b = s_scr[b, gl * gw:(gl + 1) * gw, :]
            y_b = _dot_nt(cg, s_b.astype(BF16))
            if nb > 1:
                y_b = jnp.where(row_seq_g == b, y_b, 0.0)
                bg_b = jnp.where(row_seq_n == b, bg, 0.0).astype(BF16)
            else:
                bg_b = bg.astype(BF16)
            yoff = y_b if yoff is None else yoff + y_b
            s_scr[b, gl * gw:(gl + 1) * gw, :] = decs[b][gl * gw:(gl + 1) * gw, :] * s_b + _dot(xw_t, bg_b)

        sl = slice(gl * gw, (gl + 1) * gw)
        y = ydiag + yoff * woff_x[:, sl] + dx_ref[:, sl] * xs_c[:, sl]
        yg = y * _silu(z[:, sl])
        ms = jnp.mean(yg * yg, axis=-1, keepdims=True)
        y_cols.append(yg * lax.rsqrt(ms + EPS) * nw_ref[:, sl])
    y_all = y_cols[0] if gb == 1 else jnp.concatenate(y_cols, axis=1)
    st(yssd_ref, y_all)

    @pl.when(c == n_chunks - 1)
    def _():
        sout_ref[...] = s_scr[...]


def _mixer(h, s0, tail_ssd, tail_sc, prm, *, time_major, lc, nb, gb):
    w, gn, n, p, g = prm["w"], prm["gn"], prm["n"], prm["p"], prm["g"]
    rh = (w // p) // g
    gw = rh * p
    bw = gb * gw
    bn = gb * n
    if time_major:
        seq_len, n_seq = h.shape[0], h.shape[1]
        n_chunks = 1
        assert seq_len == lc
    else:
        n_seq, seq_len = h.shape[0], h.shape[1]
        n_chunks = seq_len // lc
        assert nb == 1 and seq_len % lc == 0
    rows = lc * nb
    k_ssd = prm["cwx"].shape[0]
    k_sc = prm["scw"].shape[0]
    assert lc >= k_ssd - 1 and n_seq % nb == 0 and g % gb == 0
    assert w % bw == 0 and w % bn == 0 and (w + gn) % bn == 0 and LANES % p == 0 and gw % LANES == 0
    hr = -(-((k_ssd - 1) * nb) // SUBLANES) * SUBLANES
    off = prm["off"]

    def hspec(width, col0):
        assert col0 % width == 0
        cb0 = col0 // width
        if time_major:
            return pl.BlockSpec((lc, nb, width), lambda i, j, c: (0, i, cb0 + j))
        return pl.BlockSpec((1, lc, width), lambda i, j, c: (i, c, cb0 + j))

    def hspec_fixed(width, col0):
        cb0 = col0 // width
        if time_major:
            return pl.BlockSpec((lc, nb, width), lambda i, j, c: (0, i, cb0))
        return pl.BlockSpec((1, lc, width), lambda i, j, c: (i, c, cb0))

    def tspec(k, width, col0):
        assert col0 % width == 0
        cb0 = col0 // width
        if time_major:
            return pl.BlockSpec((k, nb, width), lambda i, j, c: (0, i, cb0 + j))
        return pl.BlockSpec((1, k, width), lambda i, j, c: (i, 0, cb0 + j))

    def ospec(width):
        if time_major:
            return pl.BlockSpec((lc, nb, width), lambda i, j, c: (0, i, j))
        return pl.BlockSpec((1, lc, width), lambda i, j, c: (i, c, j))

    def otspec(k, width):
        if time_major:
            return pl.BlockSpec((k, nb, width), lambda i, j, c: (0, i, j))
        return pl.BlockSpec((1, k, width), lambda i, j, c: (i, 0, j))

    def pspec(rows_, width):
        return pl.BlockSpec((rows_, width), lambda i, j, c: (0, j))

    def cspec(shape):
        return pl.BlockSpec(shape, lambda i, j, c: (0,) * len(shape))

    def oshape(k, width):
        return (k, n_seq, width) if time_major else (n_seq, k, width)

    r = np.arange(rows)
    same = (r[:, None] % nb) == (r[None, :] % nb)
    tri = (same & ((r[None, :] // nb) <= (r[:, None] // nb))).astype(np.float32)
    blk = same.astype(np.float32)
    hl = gb * rh
    e_mat = np.zeros((LANES, bw), np.float32)
    e_mat[np.arange(bw) // p, np.arange(bw)] = 1.0
    sel = np.zeros((g // gb, LANES, LANES), np.float32)
    for j in range(g // gb):
        sel[j, j * hl + np.arange(hl), np.arange(hl)] = 1.0

    in_specs = [
        hspec(bw, off["xs"]), hspec(bw, off["z"]), hspec(bw, off["sch"]), hspec(bw, off["scb"]),
        hspec(bw, off["scc"]), hspec(bn, off["bm"]), hspec(bn, off["cm"]), hspec_fixed(LANES, off["dt"]),
        pl.BlockSpec((nb, bw, n), lambda i, j, c: (i, j, 0)),
        tspec(k_ssd - 1, bw, 0), tspec(k_ssd - 1, bn, w), tspec(k_ssd - 1, bn, w + gn), tspec(k_sc - 1, bw, 0),
        pspec(k_ssd, bw), pspec(k_ssd, bn), pspec(k_ssd, bn), pspec(1, bw), pspec(1, bn), pspec(1, bn),
        cspec((1, LANES)), cspec((1, LANES)),
        pl.BlockSpec((1, LANES, LANES), lambda i, j, c: (j, 0, 0)),
        pspec(1, bw), pspec(1, bw), pspec(k_sc, bw),
        cspec((LANES, bw)), cspec((bw, LANES)), cspec((rows, rows)), cspec((rows, rows)), cspec((LANES, LANES)),
    ]
    out_specs = [
        ospec(bw), ospec(bw),
        pl.BlockSpec((nb, bw, n), lambda i, j, c: (i, j, 0)),
        otspec(k_ssd - 1, bw), otspec(k_ssd - 1, bn), otspec(k_ssd - 1, bn), otspec(k_sc - 1, bw),
    ]
    out_shape = [
        jax.ShapeDtypeStruct(h.shape[:2] + (w,), F32), jax.ShapeDtypeStruct(h.shape[:2] + (w,), F32),
        jax.ShapeDtypeStruct((n_seq, w, n), F32),
        jax.ShapeDtypeStruct(oshape(k_ssd - 1, w), F32), jax.ShapeDtypeStruct(oshape(k_ssd - 1, gn), F32),
        jax.ShapeDtypeStruct(oshape(k_ssd - 1, gn), F32), jax.ShapeDtypeStruct(oshape(k_sc - 1, w), F32),
    ]
    scratch = [pltpu.VMEM((nb, bw, n), F32), pltpu.VMEM((hr + rows, bw), F32), pltpu.VMEM((hr + rows, bn), F32),
               pltpu.VMEM((hr + rows, bn), F32), pltpu.VMEM((hr + rows, bw), F32)]
    kern = functools.partial(_mixer_kernel, lc=lc, nb=nb, gb=gb, rh=rh, p=p, n=n, n_chunks=n_chunks,
                             time_major=time_major)
    return pl.pallas_call(
        kern,
        grid=(n_seq // nb, g // gb, n_chunks),
        in_specs=in_specs, out_specs=out_specs, out_shape=out_shape, scratch_shapes=scratch,
        compiler_params=_cparams(("parallel", "parallel", "arbitrary")),
        name="mixer_tm" if time_major else "mixer",
    )(h, h, h, h, h, h, h, h, s0, tail_ssd, tail_ssd, tail_ssd, tail_sc,
      prm["cwx"], prm["cwb"], prm["cwc"], prm["cbx"], prm["cbb"], prm["cbc"],
      prm["dtb"], prm["alog"], jnp.asarray(sel), prm["dx"], prm["nw"], prm["scw"],
      jnp.asarray(e_mat), jnp.asarray(e_mat.T), jnp.asarray(tri), jnp.asarray(blk),
      jnp.asarray(np.eye(LANES, dtype=np.float32)))


def _attn_kernel(q_ref, k_ref, v_ref, o_ref, kb_ref, vb_ref, *, heads):
    @pl.when(pl.program_id(1) == 0)
    def _():
        kb_ref[...] = k_ref[0].astype(BF16)
        vb_ref[...] = v_ref[0].astype(BF16)

    d = q_ref.shape[-1]
    hd = d // heads
    scale = 1.0 / math.sqrt(hd)
    for h in range(heads):
        sl = slice(h * hd, (h + 1) * hd)
        s = _dot_nt(q_ref[0, :, sl].astype(BF16), kb_ref[:, sl]) * scale
        m = jnp.max(s, axis=-1, keepdims=True)
        pexp = jnp.exp(s - m)
        prob = pexp / jnp.sum(pexp, axis=-1, keepdims=True)
        o_ref[0, :, sl] = _dot(prob.astype(BF16), vb_ref[:, sl])


def _attention(q, k, v, heads):
    b, l, d = q.shape
    m = k.shape[1]
    tq = _tile(l, 512, SUBLANES)
    return pl.pallas_call(
        functools.partial(_attn_kernel, heads=heads),
        grid=(b, l // tq),
        in_specs=[pl.BlockSpec((1, tq, d), lambda i, j: (i, j, 0)),
                  pl.BlockSpec((1, m, d), lambda i, j: (i, 0, 0)),
                  pl.BlockSpec((1, m, d), lambda i, j: (i, 0, 0))],
        out_specs=pl.BlockSpec((1, tq, d), lambda i, j: (i, j, 0)),
        out_shape=jax.ShapeDtypeStruct((b, l, d), F32),
        scratch_shapes=[pltpu.VMEM((m, d), BF16), pltpu.VMEM((m, d), BF16)],
        compiler_params=_cparams(("parallel", "arbitrary")),
        name="mem_attn",
    )(q, k, v)


def _top_extract(s, iota_f, k):
    rank = jnp.full(s.shape, 255.0, F32)
    vals = []
    for r in range(k):
        m = jnp.max(s, axis=0, keepdims=True)
        idx = jnp.min(jnp.where(s == m, iota_f, 1e9), axis=0, keepdims=True)
        hit = iota_f == idx
        rank = jnp.where(hit, float(r), rank)
        s = jnp.where(hit, NEG_INF, s)
        vals.append(m)
    return rank, vals


def _peer_route_kernel(x_ref, g_ref, wq_ref, sk_ref, xnt_ref, r_ref, q_scr, *, heads, k):
    x = x_ref[...]
    ms = jnp.mean(x * x, axis=-1, keepdims=True)
    xn = x * lax.rsqrt(ms + EPS) * g_ref[...]
    xnt = xn.T.astype(BF16)
    xnt_ref[...] = xnt
    q_scr[...] = _dot(wq_ref[...], xnt)
    nk, half = sk_ref.shape[1], sk_ref.shape[2]
    tm = x.shape[0]
    iota_k = lax.broadcasted_iota(jnp.int32, (nk, tm), 0).astype(F32)

    kh = k // 2
    n_cand = k + (kh - 1) * kh + kh
    row_id = lax.broadcasted_iota(jnp.int32, (n_cand, tm), 0)
    pos = jnp.zeros((n_cand, tm), F32)
    pos = jnp.where(row_id < k, row_id.astype(F32), pos)
    for p_ in range(1, kh):
        lo = k + (p_ - 1) * kh
        pos = jnp.where((row_id >= lo) & (row_id < lo + kh), (row_id - lo + p_ * k).astype(F32), pos)
    lo = k + (kh - 1) * kh
    pos = jnp.where(row_id >= lo, ((row_id - lo + kh) * k).astype(F32), pos)

    def head_body(h, carry):
        s_half = []
        for half_i in range(2):
            row0 = pl.multiple_of((h * 2 + half_i) * half, half)
            qh = q_scr[pl.ds(row0, half), :].astype(BF16)
            s_half.append(_dot(sk_ref[h * 2 + half_i], qh))
        rank1, a = _top_extract(s_half[0], iota_k, k)
        rank2, b = _top_extract(s_half[1], iota_k, k)
        b_lo = jnp.concatenate(b[:kh], axis=0)
        cand = jnp.concatenate([a[0] + jnp.concatenate(b, axis=0)]
                               + [a[p_] + b_lo for p_ in range(1, kh)]
                               + [jnp.concatenate(a[kh:], axis=0) + b[0]], axis=0)
        c_max = a[0] + b[0]
        sel = jnp.zeros(cand.shape, F32)
        cur = cand
        for _ in range(k):
            m = jnp.max(cur, axis=0, keepdims=True)
            idx = jnp.min(jnp.where(cur == m, pos, 1e9), axis=0, keepdims=True)
            hit = pos == idx
            sel = jnp.where(hit, 1.0, sel)
            cur = jnp.where(hit, NEG_INF, cur)
        zsum = jnp.sum(sel * jnp.exp(cand - c_max), axis=0, keepdims=True)
        q_rows = [jnp.sum(sel[0:k], axis=0, keepdims=True)]
        for p_ in range(1, kh):
            lo_ = k + (p_ - 1) * kh
            q_rows.append(jnp.sum(sel[lo_:lo_ + kh], axis=0, keepdims=True))
        lo_ = k + (kh - 1) * kh
        for p_ in range(kh, k):
            q_rows.append(sel[lo_ + p_ - kh:lo_ + p_ - kh + 1])
        qmap = jnp.zeros((nk, tm), F32)
        for p_ in range(k):
            qmap = jnp.where(rank1 == float(p_), q_rows[p_], qmap)
        r_ref[0, h] = rank2
        r_ref[1, h] = jnp.exp(s_half[1] - b[0])
        r_ref[2, h] = qmap
        r_ref[3, h] = jnp.exp(s_half[0] - a[0]) * (1.0 / zsum)
        return carry

    lax.fori_loop(0, heads, head_body, 0)


def _peer_route(x, g, wq_t, sk):
    t, d = x.shape
    tm = LANES
    heads2, nk, half = sk.shape
    heads = heads2 // 2
    return pl.pallas_call(
        functools.partial(_peer_route_kernel, heads=heads, k=PEER_TOPK),
        grid=(t // tm,),
        in_specs=[pl.BlockSpec((tm, d), lambda i: (i, 0)),
                  pl.BlockSpec((1, d), lambda i: (0, 0)),
                  pl.BlockSpec(wq_t.shape, lambda i: (0, 0)),
                  pl.BlockSpec(sk.shape, lambda i: (0, 0, 0))],
        out_specs=[pl.BlockSpec((d, tm), lambda i: (0, i)),
                   pl.BlockSpec((4, heads, nk, tm), lambda i: (0, 0, 0, i))],
        out_shape=[jax.ShapeDtypeStruct((d, t), BF16), jax.ShapeDtypeStruct((4, heads, nk, t), F32)],
        scratch_shapes=[pltpu.VMEM((wq_t.shape[0], tm), F32)],
        compiler_params=_cparams(("parallel",)),
        name="peer_route",
    )(x, g.reshape(1, d), wq_t, sk)


def _peer_main_kernel(xnt_ref, r_ref, u_ref, vt_ref, o_ref, acc_ref, ht_ref, p_ref, *, heads, nk):
    e = pl.program_id(1)

    @pl.when(e == 0)
    def _():
        acc_ref[...] = jnp.zeros_like(acc_ref)

    ht_ref[...] = _dot(u_ref[...], xnt_ref[...])
    ib = u_ref.shape[0] // nk
    for ii in range(ib):
        i = e * ib + ii
        act = _gelu_exact(ht_ref[ii * nk:(ii + 1) * nk, :])
        gate = None
        for h in range(heads):
            q_row = r_ref[2, h, pl.ds(i, 1), :]
            e1_row = r_ref[3, h, pl.ds(i, 1), :]
            t = jnp.where(r_ref[0, h] < q_row, r_ref[1, h], 0.0) * e1_row
            gate = t if gate is None else gate + t
        p_ref[ii * nk:(ii + 1) * nk, :] = (gate * act).astype(BF16)
    acc_ref[...] += _dot(vt_ref[...], p_ref[...])

    @pl.when(e == pl.num_programs(1) - 1)
    def _():
        o_ref[...] = acc_ref[...].T


def _peer_main(xnt, r, u, vt, nk):
    d, t = xnt.shape
    n_exp = u.shape[0]
    heads = r.shape[1]
    tm = _tile(t, 512, LANES)
    te = _tile(n_exp, 512, nk)
    return pl.pallas_call(
        functools.partial(_peer_main_kernel, heads=heads, nk=nk),
        grid=(t // tm, n_exp // te),
        in_specs=[pl.BlockSpec((d, tm), lambda i, j: (0, i)),
                  pl.BlockSpec((4, heads, nk, tm), lambda i, j: (0, 0, 0, i)),
                  pl.BlockSpec((te, d), lambda i, j: (j, 0)),
                  pl.BlockSpec((d, te), lambda i, j: (0, j))],
        out_specs=pl.BlockSpec((tm, d), lambda i, j: (i, 0)),
        out_shape=jax.ShapeDtypeStruct((t, d), F32),
        scratch_shapes=[pltpu.VMEM((d, tm), F32), pltpu.VMEM((te, tm), F32), pltpu.VMEM((te, tm), BF16)],
        compiler_params=_cparams(("parallel", "arbitrary")),
        name="peer_main",
    )(xnt, r, u, vt)


def _add_rms_kernel(x_ref, d_ref, g_ref, o_ref):
    x = x_ref[...] + d_ref[...]
    ms = jnp.mean(x * x, axis=-1, keepdims=True)
    o_ref[...] = x * lax.rsqrt(ms + EPS) * g_ref[...]


def _add_rms(x, delta, g):
    t, d = x.shape
    tm = _tile(t, 512, SUBLANES)
    return pl.pallas_call(
        _add_rms_kernel,
        grid=(t // tm,),
        in_specs=[pl.BlockSpec((tm, d), lambda i: (i, 0)), pl.BlockSpec((tm, d), lambda i: (i, 0)),
                  pl.BlockSpec((1, d), lambda i: (0, 0))],
        out_specs=pl.BlockSpec((tm, d), lambda i: (i, 0)),
        out_shape=jax.ShapeDtypeStruct((t, d), F32),
        compiler_params=_cparams(("parallel",)),
        name="add_rms",
    )(x, delta, g.reshape(1, d))


def _pad_lanes(v):
    return jnp.pad(v, (0, LANES - v.shape[0])).reshape(1, LANES)


def _layer_params(i, dims, norm_mix, w_in, ssd_conv_w, ssd_conv_b, ssd_dt_bias, ssd_a_log, ssd_d, ssd_norm,
                  sc_conv_w, w_out, norm_mem_q, norm_mem_kv, w_mem_q, w_mem_k, w_mem_v, w_mem_o, norm_ffn,
                  w_peer_q, peer_sub_keys, peer_u, peer_v):
    w, gn, hh, scw = dims["w"], dims["gn"], dims["heads"], dims["scw"]
    cd = w + 2 * gn
    s1, s2, s3 = w, w + cd, w + cd + hh
    s4, s5 = s3 + scw, s3 + 2 * scw
    wi = w_in[i]
    w_in_p = jnp.concatenate(
        [wi[:, s1:s1 + w], wi[:, :s1], wi[:, s3:s4], wi[:, s4:s5], wi[:, s5:], wi[:, s1 + w:s1 + w + gn],
         wi[:, s1 + w + gn:s2], jnp.pad(wi[:, s2:s3], ((0, 0), (0, LANES - hh)))], axis=1).astype(BF16)
    off = {"xs": 0, "z": w, "sch": 2 * w, "scb": 3 * w, "scc": 4 * w, "bm": 5 * w, "cm": 5 * w + gn,
           "dt": 5 * w + 2 * gn}
    cw, cb = ssd_conv_w[i], ssd_conv_b[i]
    mix = dict(dims, off=off,
               cwx=cw[:, :w], cwb=cw[:, w:w + gn], cwc=cw[:, w + gn:],
               cbx=cb[:w].reshape(1, w), cbb=cb[w:w + gn].reshape(1, gn), cbc=cb[w + gn:].reshape(1, gn),
               dtb=_pad_lanes(ssd_dt_bias[i]), alog=_pad_lanes(ssd_a_log[i]),
               dx=jnp.repeat(ssd_d[i], dims["p"]).reshape(1, w), nw=ssd_norm[i].reshape(1, w), scw=sc_conv_w[i])
    ph, _, nk, half = peer_sub_keys[i].shape
    return dict(
        norm_mix=norm_mix[i], w_in=w_in_p, mix=mix,
        w_out_ssd=w_out[i][:w].astype(BF16), w_out_sc=w_out[i][w:].astype(BF16),
        norm_mem_q=norm_mem_q[i], norm_mem_kv=norm_mem_kv[i], w_mem_q=w_mem_q[i].astype(BF16),
        w_mem_kv=jnp.concatenate([w_mem_k[i], w_mem_v[i]], axis=1).astype(BF16), w_mem_o=w_mem_o[i].astype(BF16),
        norm_ffn=norm_ffn[i], w_peer_q_t=w_peer_q[i].T.astype(BF16),
        sub_keys=peer_sub_keys[i].reshape(ph * 2, nk, half).astype(BF16),
        peer_u=peer_u[i].astype(BF16), peer_v_t=peer_v[i].T.astype(BF16), nk=nk)


def _attn_peer(x1, mem_k, mem_v, lp, seq_shape, mem_heads):
    b, l = seq_shape
    d = x1.shape[1]
    q = _rms_matmul(x1, lp["norm_mem_q"], lp["w_mem_q"])
    o = _attention(q.reshape(b, l, d), mem_k, mem_v, mem_heads).reshape(b * l, d)
    x2 = _matmul_res([o], [lp["w_mem_o"]], x1)
    xnt, route = _peer_route(x2, lp["norm_ffn"], lp["w_peer_q_t"], lp["sub_keys"])
    delta = _peer_main(xnt, route, lp["peer_u"], lp["peer_v_t"], lp["nk"])
    return x2, delta


def kernel(x_prompt, x_sample, mem_prompt, state_ssm, state_ssd_conv, state_short_conv, cache_mem_k, cache_mem_v,
           norm_mix, w_in, ssd_conv_w, ssd_conv_b, ssd_dt_bias, ssd_a_log, ssd_d, ssd_norm, sc_conv_w, w_out,
           norm_mem_q, norm_mem_kv, w_mem_q, w_mem_k, w_mem_v, w_mem_o, norm_ffn, w_peer_q, peer_sub_keys,
           peer_u, peer_v, norm_final):
    bp, lp_, d = x_prompt.shape
    bs, ls, _ = x_sample.shape
    depth, _, heads, p, n = state_ssm.shape
    k_ssd = state_ssd_conv.shape[2] + 1
    cd = state_ssd_conv.shape[3]
    k_sc = state_short_conv.shape[2] + 1
    scw = state_short_conv.shape[3]
    w = heads * p
    gn = (cd - w) // 2
    g = gn // n
    mem_tokens, mem_heads, mem_hd = cache_mem_k.shape[2:]
    assert scw == w and d == mem_heads * mem_hd
    dims = dict(w=w, gn=gn, n=n, p=p, g=g, heads=heads, scw=scw)
    lc_p = min(SSD_CHUNK, lp_)
    nb_s = 16 if (bs % 16 == 0) else SUBLANES
    assert lp_ % lc_p == 0 and ls <= SSD_CHUNK and bs % nb_s == 0

    hp = x_prompt.reshape(bp * lp_, d)
    hs = jnp.transpose(x_sample, (1, 0, 2)).reshape(ls * bs, d)
    outs = {k_: [] for k_ in ("p_ssm", "p_ssd", "p_sc", "p_mk", "p_mv", "s_ssm", "s_ssd", "s_sc")}
    for i in range(depth):
        lp = _layer_params(i, dims, norm_mix, w_in, ssd_conv_w, ssd_conv_b, ssd_dt_bias, ssd_a_log, ssd_d,
                           ssd_norm, sc_conv_w, w_out, norm_mem_q, norm_mem_kv, w_mem_q, w_mem_k, w_mem_v,
                           w_mem_o, norm_ffn, w_peer_q, peer_sub_keys, peer_u, peer_v)
        kv = _rms_matmul(mem_prompt.reshape(bp * mem_tokens, d), lp["norm_mem_kv"], lp["w_mem_kv"])
        mk = kv[:, :d].reshape(bp, mem_tokens, d)
        mv = kv[:, d:].reshape(bp, mem_tokens, d)
        h = _rms_matmul(hp, lp["norm_mix"], lp["w_in"]).reshape(bp, lp_, -1)
        y_ssd, y_sc, n_ssm, ntx, ntb, ntc, ntsc = _mixer(
            h, jnp.zeros((bp, w, n), F32), jnp.zeros((bp, k_ssd - 1, cd), F32), jnp.zeros((bp, k_sc - 1, scw), F32),
            lp["mix"], time_major=False, lc=lc_p, nb=1, gb=g)
        x1 = _matmul_res([y_ssd.reshape(bp * lp_, w), y_sc.reshape(bp * lp_, w)],
                         [lp["w_out_ssd"], lp["w_out_sc"]], hp)
        x2, delta = _attn_peer(x1, mk, mv, lp, (bp, lp_), mem_heads)
        if i == depth - 1:
            hp = _add_rms(x2, delta, norm_final)
        else:
            hp = x2 + delta
        outs["p_ssm"].append(n_ssm.reshape(bp, heads, p, n))
        outs["p_ssd"].append(jnp.concatenate([ntx, ntb, ntc], axis=-1))
        outs["p_sc"].append(ntsc)
        outs["p_mk"].append(mk.reshape(bp, mem_tokens, mem_heads, mem_hd))
        outs["p_mv"].append(mv.reshape(bp, mem_tokens, mem_heads, mem_hd))
        h = _rms_matmul(hs, lp["norm_mix"], lp["w_in"]).reshape(ls, bs, -1)
        y_ssd, y_sc, m_ssm, ntx, ntb, ntc, ntsc = _mixer(
            h, state_ssm[i].reshape(bs, w, n), jnp.transpose(state_ssd_conv[i], (1, 0, 2)),
            jnp.transpose(state_short_conv[i], (1, 0, 2)), lp["mix"], time_major=True, lc=ls, nb=nb_s, gb=1)
        x1 = _matmul_res([y_ssd.reshape(ls * bs, w), y_sc.reshape(ls * bs, w)],
                         [lp["w_out_ssd"], lp["w_out_sc"]], hs)
        x1 = jnp.transpose(x1.reshape(ls, bs, d), (1, 0, 2)).reshape(bs * ls, d)
        x2, delta = _attn_peer(x1, cache_mem_k[i].reshape(bs, mem_tokens, d),
                               cache_mem_v[i].reshape(bs, mem_tokens, d), lp, (bs, ls), mem_heads)
        if i == depth - 1:
            hs_bm = _add_rms(x2, delta, norm_final)
        else:
            hs_bm = x2 + delta
        hs = jnp.transpose(hs_bm.reshape(bs, ls, d), (1, 0, 2)).reshape(ls * bs, d) if i < depth - 1 else hs_bm
        outs["s_ssm"].append(m_ssm.reshape(bs, heads, p, n))
        outs["s_ssd"].append(jnp.transpose(jnp.concatenate([ntx, ntb, ntc], axis=-1), (1, 0, 2)))
        outs["s_sc"].append(jnp.transpose(ntsc, (1, 0, 2)))
    y_prompt = hp.reshape(bp, lp_, d)
    y_sample = hs.reshape(bs, ls, d)
    return (y_prompt, y_sample, jnp.stack(outs["p_ssm"]), jnp.stack(outs["p_ssd"]), jnp.stack(outs["p_sc"]),
            jnp.stack(outs["p_mk"]), jnp.stack(outs["p_mv"]), jnp.stack(outs["s_ssm"]), jnp.stack(outs["s_ssd"]),
            jnp.stack(outs["s_sc"]))
```

```python
import functools
import math

import jax
import jax.numpy as jnp
import numpy as np
from jax import lax
from jax.experimental import pallas as pl
from jax.experimental.pallas import tpu as pltpu

F32 = jnp.float32
BF16 = jnp.bfloat16

EPS = 1e-6
SSD_CHUNK = 128
PEER_TOPK = 16
LANES = 128
SUBLANES = 8
VMEM_LIMIT_BYTES = 56 * 1024 * 1024
NEG_INF = float("-inf")
IN_PROJ_COL_TILE = 512


def _cparams(sem):
    return pltpu.CompilerParams(dimension_semantics=sem, vmem_limit_bytes=VMEM_LIMIT_BYTES)


def _tile(n, pref, mult):
    if n <= pref:
        return n
    t = (pref // mult) * mult
    while t >= mult:
        if n % t == 0:
            return t
        t -= mult
    return n


def _hdot(a, b):
    return jnp.dot(a, b, precision=lax.Precision.HIGHEST, preferred_element_type=F32)


def _hdot_nt(a, b):
    return lax.dot_general(a, b, (((1,), (1,)), ((), ())), precision=lax.Precision.HIGHEST,
                           preferred_element_type=F32)


def _dot(a, b):
    return jnp.dot(a, b, preferred_element_type=F32)


def _dot_nt(a, b):
    return lax.dot_general(a, b, (((1,), (1,)), ((), ())), preferred_element_type=F32)


def _silu(x):
    return x * (1.0 / (1.0 + jnp.exp(-x)))


def _softplus(x):
    return jnp.maximum(x, 0.0) + jnp.log1p(jnp.exp(-jnp.abs(x)))


def _gelu_exact(x):
    return 0.5 * x * (1.0 + lax.erf(x * (1.0 / math.sqrt(2.0))))


def _rms_matmul_kernel(x_ref, g_ref, w_ref, o_ref, xn_ref):
    @pl.when(pl.program_id(1) == 0)
    def _():
        x = x_ref[...]
        ms = jnp.mean(x * x, axis=-1, keepdims=True)
        xn_ref[...] = (x * lax.rsqrt(ms + EPS) * g_ref[...]).astype(BF16)

    o_ref[...] = _dot(xn_ref[...], w_ref[...]).astype(o_ref.dtype)


def _rms_matmul(x, g, w, out_dtype=F32):
    t, d = x.shape
    n = w.shape[1]
    tm = _tile(t, 1024, SUBLANES)
    tn = _tile(n, 512, LANES)
    return pl.pallas_call(
        _rms_matmul_kernel,
        grid=(t // tm, n // tn),
        in_specs=[pl.BlockSpec((tm, d), lambda i, j: (i, 0)),
                  pl.BlockSpec((1, d), lambda i, j: (0, 0)),
                  pl.BlockSpec((d, tn), lambda i, j: (0, j))],
        out_specs=pl.BlockSpec((tm, tn), lambda i, j: (i, j)),
        out_shape=jax.ShapeDtypeStruct((t, n), out_dtype),
        scratch_shapes=[pltpu.VMEM((tm, d), BF16)],
        compiler_params=_cparams(("parallel", "arbitrary")),
        name="rms_matmul",
    )(x, g.reshape(1, d), w)


def _matmul_res_kernel(*refs, n_a):
    a_refs = refs[:n_a]
    w_refs = refs[n_a:2 * n_a]
    r_ref, o_ref = refs[2 * n_a], refs[2 * n_a + 1]
    ab_refs = refs[2 * n_a + 2:]

    @pl.when(pl.program_id(1) == 0)
    def _():
        for a_ref, ab_ref in zip(a_refs, ab_refs):
            ab_ref[...] = a_ref[...].astype(BF16)

    acc = r_ref[...]
    for ab_ref, w_ref in zip(ab_refs, w_refs):
        acc = acc + _dot(ab_ref[...], w_ref[...])
    o_ref[...] = acc


def _matmul_res(a_list, w_list, res):
    t, d = res.shape
    tm = _tile(t, 1024, SUBLANES)
    tn = _tile(d, 512, LANES)
    n_a = len(a_list)
    in_specs = ([pl.BlockSpec((tm, a.shape[1]), lambda i, j: (i, 0)) for a in a_list]
                + [pl.BlockSpec((w.shape[0], tn), lambda i, j: (0, j)) for w in w_list]
                + [pl.BlockSpec((tm, tn), lambda i, j: (i, j))])
    return pl.pallas_call(
        functools.partial(_matmul_res_kernel, n_a=n_a),
        grid=(t // tm, d // tn),
        in_specs=in_specs,
        out_specs=pl.BlockSpec((tm, tn), lambda i, j: (i, j)),
        out_shape=jax.ShapeDtypeStruct((t, d), F32),
        scratch_shapes=[pltpu.VMEM((tm, a.shape[1]), BF16) for a in a_list],
        compiler_params=_cparams(("parallel", "arbitrary")),
        name="matmul_res",
    )(*a_list, *w_list, res)


def _mixer_kernel(xs_ref, z_ref, sch_ref, scb_ref, scc_ref, bm_ref, cm_ref, dt_ref,
                  s0_ref, tx_ref, tb_ref, tc_ref, tsc_ref,
                  cwx_ref, cwb_ref, cwc_ref, cbx_ref, cbb_ref, cbc_ref,
                  dtb_ref, alog_ref, sel_ref, dx_ref, nw_ref, scw_ref,
                  e_ref, et_ref, tri_ref, blk_ref, eye_ref,
                  yssd_ref, ysc_ref, sout_ref, ntx_ref, ntb_ref, ntc_ref, ntsc_ref,
                  s_scr, padx, padb, padc, padu,
                  *, lc, nb, gb, rh, p, n, n_chunks, time_major):
    rows = lc * nb
    gw = rh * p
    hr = padx.shape[0] - rows
    k_ssd = cwx_ref.shape[0]
    k_sc = scw_ref.shape[0]
    c = pl.program_id(2)

    def ld(ref):
        if time_major:
            return ref[...].reshape(ref.shape[0] * ref.shape[1], ref.shape[2])
        return ref[0]

    def st(ref, val):
        if time_major:
            ref[...] = val.reshape(ref.shape)
        else:
            ref[0] = val

    @pl.when(c == 0)
    def _():
        s_scr[...] = s0_ref[...]
        padx[hr - (k_ssd - 1) * nb:hr, :] = ld(tx_ref)
        padb[hr - (k_ssd - 1) * nb:hr, :] = ld(tb_ref)
        padc[hr - (k_ssd - 1) * nb:hr, :] = ld(tc_ref)
        padu[hr - (k_sc - 1) * nb:hr, :] = ld(tsc_ref)

    def conv(pad, x, w_ref, nk):
        pad[hr:hr + rows, :] = x
        acc = None
        for j in range(nk):
            off = hr - (nk - 1 - j) * nb
            term = pad[off:off + rows, :] * w_ref[j:j + 1, :]
            acc = term if acc is None else acc + term
        tail = pad[hr + rows - (nk - 1) * nb:hr + rows, :]
        if n_chunks > 1:
            pad[hr - (nk - 1) * nb:hr, :] = tail
        return acc, tail

    xs_pre, tail_x = conv(padx, ld(xs_ref), cwx_ref, k_ssd)
    b_pre, tail_b = conv(padb, ld(bm_ref), cwb_ref, k_ssd)
    c_pre, tail_c = conv(padc, ld(cm_ref), cwc_ref, k_ssd)
    cv, tail_u = conv(padu, ld(scc_ref) * ld(sch_ref), scw_ref, k_sc)
    st(ntx_ref, tail_x)
    st(ntb_ref, tail_b)
    st(ntc_ref, tail_c)
    st(ntsc_ref, tail_u)
    st(ysc_ref, ld(scb_ref) * cv)

    xs_c = _silu(xs_pre + cbx_ref[...])
    b_c = _silu(b_pre + cbb_ref[...])
    c_c = _silu(c_pre + cbc_ref[...])

    dt_all = _softplus(ld(dt_ref) + dtb_ref[...])
    a_all = dt_all * (-jnp.exp(alog_ref[...]))
    sel = sel_ref[0]
    dt = _hdot(dt_all, sel)
    a = _hdot(a_all, sel)
    tri = tri_ref[...]
    a_cs = _hdot(tri, a)
    a_tot = _hdot(blk_ref[...], a)
    e = e_ref[...]
    dt_x = _hdot(dt, e)
    wst_x = _hdot(jnp.exp(a_tot - a_cs), e)
    woff_x = _hdot(jnp.exp(a_cs), e)
    xdt = xs_c * dt_x
    xw = xdt * wst_x
    eye = eye_ref[...]
    a_cs_t = _hdot_nt(eye, a_cs)
    tri_mask = tri > 0.5
    lane_id = lax.broadcasted_iota(jnp.int32, (rows, LANES), 1)
    row_seq_g = lax.broadcasted_iota(jnp.int32, (rows, gw), 0) % nb
    row_seq_n = lax.broadcasted_iota(jnp.int32, (rows, n), 0) % nb

    decs = []
    for b in range(nb):
        row_b = jnp.broadcast_to(a_tot[b:b + 1, :], (n, LANES))
        d_b = jnp.exp(_hdot_nt(eye, row_b))
        decs.append(_hdot(et_ref[...], d_b))

    z = ld(z_ref)
    y_cols = []
    for gl in range(gb):
        cg = c_c[:, gl * n:(gl + 1) * n].astype(BF16)
        bg = b_c[:, gl * n:(gl + 1) * n]
        cb = _dot_nt(cg, bg.astype(BF16))
        ydiag = []
        for pr in range(gw // LANES):
            xpair = xdt[:, gl * gw + pr * LANES:gl * gw + (pr + 1) * LANES]
            acc = None
            for sub in range(LANES // p):
                hl = gl * rh + pr * (LANES // p) + sub
                col = a_cs[:, hl:hl + 1]
                row = a_cs_t[hl:hl + 1, :]
                lmat = jnp.exp(jnp.where(tri_mask, col - row, NEG_INF))
                m = (cb * lmat).astype(BF16)
                xm = jnp.where((lane_id >= sub * p) & (lane_id < (sub + 1) * p), xpair, 0.0).astype(BF16)
                t = _dot(m, xm)
                acc = t if acc is None else acc + t
            ydiag.append(acc)
        ydiag = ydiag[0] if len(ydiag) == 1 else jnp.concatenate(ydiag, axis=1)

        xw_t = xw[:, gl * gw:(gl + 1) * gw].T.astype(BF16)
        yoff = None
        for b in range(nb):
            s_b = s_scr[b, gl * gw:(gl + 1) * gw, :]
            y_b = _dot_nt(cg, s_b.astype(BF16))
            if nb > 1:
                y_b = jnp.where(row_seq_g == b, y_b, 0.0)
                bg_b = jnp.where(row_seq_n == b, bg, 0.0).astype(BF16)
            else:
                bg_b = bg.astype(BF16)
            yoff = y_b if yoff is None else yoff + y_b
            s_scr[b, gl * gw:(gl + 1) * gw, :] = decs[b][gl * gw:(gl + 1) * gw, :] * s_b + _dot(xw_t, bg_b)

        sl = slice(gl * gw, (gl + 1) * gw)
        y = ydiag + yoff * woff_x[:, sl] + dx_ref[:, sl] * xs_c[:, sl]
        yg = y * _silu(z[:, sl])
        ms = jnp.mean(yg * yg, axis=-1, keepdims=True)
        y_cols.append(yg * lax.rsqrt(ms + EPS) * nw_ref[:, sl])
    y_all = y_cols[0] if gb == 1 else jnp.concatenate(y_cols, axis=1)
    st(yssd_ref, y_all)

    @pl.when(c == n_chunks - 1)
    def _():
        sout_ref[...] = s_scr[...]


def _mixer(h, s0, tail_ssd, tail_sc, prm, *, time_major, lc, nb, gb):
    w, gn, n, p, g = prm["w"], prm["gn"], prm["n"], prm["p"], prm["g"]
    rh = (w // p) // g
    gw = rh * p
    bw = gb * gw
    bn = gb * n
    if time_major:
        seq_len, n_seq = h.shape[0], h.shape[1]
        n_chunks = 1
        assert seq_len == lc
    else:
        n_seq, seq_len = h.shape[0], h.shape[1]
        n_chunks = seq_len // lc
        assert nb == 1 and seq_len % lc == 0
    rows = lc * nb
    k_ssd = prm["cwx"].shape[0]
    k_sc = prm["scw"].shape[0]
    assert lc >= k_ssd - 1 and n_seq % nb == 0 and g % gb == 0
    assert w % bw == 0 and w % bn == 0 and (w + gn) % bn == 0 and LANES % p == 0 and gw % LANES == 0
    hr = -(-((k_ssd - 1) * nb) // SUBLANES) * SUBLANES
    off = prm["off"]

    def hspec(width, col0):
        assert col0 % width == 0
        cb0 = col0 // width
        if time_major:
            return pl.BlockSpec((lc, nb, width), lambda i, j, c: (0, i, cb0 + j))
        return pl.BlockSpec((1, lc, width), lambda i, j, c: (i, c, cb0 + j))

    def hspec_fixed(width, col0):
        cb0 = col0 // width
        if time_major:
            return pl.BlockSpec((lc, nb, width), lambda i, j, c: (0, i, cb0))
        return pl.BlockSpec((1, lc, width), lambda i, j, c: (i, c, cb0))

    def tspec(k, width, col0):
        assert col0 % width == 0
        cb0 = col0 // width
        if time_major:
            return pl.BlockSpec((k, nb, width), lambda i, j, c: (0, i, cb0 + j))
        return pl.BlockSpec((1, k, width), lambda i, j, c: (i, 0, cb0 + j))

    def ospec(width):
        if time_major:
            return pl.BlockSpec((lc, nb, width), lambda i, j, c: (0, i, j))
        return pl.BlockSpec((1, lc, width), lambda i, j, c: (i, c, j))

    def otspec(k, width):
        if time_major:
            return pl.BlockSpec((k, nb, width), lambda i, j, c: (0, i, j))
        return pl.BlockSpec((1, k, width), lambda i, j, c: (i, 0, j))

    def pspec(rows_, width):
        return pl.BlockSpec((rows_, width), lambda i, j, c: (0, j))

    def cspec(shape):
        return pl.BlockSpec(shape, lambda i, j, c: (0,) * len(shape))

    def oshape(k, width):
        return (k, n_seq, width) if time_major else (n_seq, k, width)

    r = np.arange(rows)
    same = (r[:, None] % nb) == (r[None, :] % nb)
    tri = (same & ((r[None, :] // nb) <= (r[:, None] // nb))).astype(np.float32)
    blk = same.astype(np.float32)
    hl = gb * rh
    e_mat = np.zeros((LANES, bw), np.float32)
    e_mat[np.arange(bw) // p, np.arange(bw)] = 1.0
    sel = np.zeros((g // gb, LANES, LANES), np.float32)
    for j in range(g // gb):
        sel[j, j * hl + np.arange(hl), np.arange(hl)] = 1.0

    in_specs = [
        hspec(bw, off["xs"]), hspec(bw, off["z"]), hspec(bw, off["sch"]), hspec(bw, off["scb"]),
        hspec(bw, off["scc"]), hspec(bn, off["bm"]), hspec(bn, off["cm"]), hspec_fixed(LANES, off["dt"]),
        pl.BlockSpec((nb, bw, n), lambda i, j, c: (i, j, 0)),
        tspec(k_ssd - 1, bw, 0), tspec(k_ssd - 1, bn, w), tspec(k_ssd - 1, bn, w + gn), tspec(k_sc - 1, bw, 0),
        pspec(k_ssd, bw), pspec(k_ssd, bn), pspec(k_ssd, bn), pspec(1, bw), pspec(1, bn), pspec(1, bn),
        cspec((1, LANES)), cspec((1, LANES)),
        pl.BlockSpec((1, LANES, LANES), lambda i, j, c: (j, 0, 0)),
        pspec(1, bw), pspec(1, bw), pspec(k_sc, bw),
        cspec((LANES, bw)), cspec((bw, LANES)), cspec((rows, rows)), cspec((rows, rows)), cspec((LANES, LANES)),
    ]
    out_specs = [
        ospec(bw), ospec(bw),
        pl.BlockSpec((nb, bw, n), lambda i, j, c: (i, j, 0)),
        otspec(k_ssd - 1, bw), otspec(k_ssd - 1, bn), otspec(k_ssd - 1, bn), otspec(k_sc - 1, bw),
    ]
    out_shape = [
        jax.ShapeDtypeStruct(h.shape[:2] + (w,), F32), jax.ShapeDtypeStruct(h.shape[:2] + (w,), F32),
        jax.ShapeDtypeStruct((n_seq, w, n), F32),
        jax.ShapeDtypeStruct(oshape(k_ssd - 1, w), F32), jax.ShapeDtypeStruct(oshape(k_ssd - 1, gn), F32),
        jax.ShapeDtypeStruct(oshape(k_ssd - 1, gn), F32), jax.ShapeDtypeStruct(oshape(k_sc - 1, w), F32),
    ]
    scratch = [pltpu.VMEM((nb, bw, n), F32), pltpu.VMEM((hr + rows, bw), F32), pltpu.VMEM((hr + rows, bn), F32),
               pltpu.VMEM((hr + rows, bn), F32), pltpu.VMEM((hr + rows, bw), F32)]
    kern = functools.partial(_mixer_kernel, lc=lc, nb=nb, gb=gb, rh=rh, p=p, n=n, n_chunks=n_chunks,
                             time_major=time_major)
    return pl.pallas_call(
        kern,
        grid=(n_seq // nb, g // gb, n_chunks),
        in_specs=in_specs, out_specs=out_specs, out_shape=out_shape, scratch_shapes=scratch,
        compiler_params=_cparams(("parallel", "parallel", "arbitrary")),
        name="mixer_tm" if time_major else "mixer",
    )(h, h, h, h, h, h, h, h, s0, tail_ssd, tail_ssd, tail_ssd, tail_sc,
      prm["cwx"], prm["cwb"], prm["cwc"], prm["cbx"], prm["cbb"], prm["cbc"],
      prm["dtb"], prm["alog"], jnp.asarray(sel), prm["dx"], prm["nw"], prm["scw"],
      jnp.asarray(e_mat), jnp.asarray(e_mat.T), jnp.asarray(tri), jnp.asarray(blk),
      jnp.asarray(np.eye(LANES, dtype=np.float32)))


def _attn_kernel(q_ref, k_ref, v_ref, o_ref, kb_ref, vb_ref, *, heads):
    d = q_ref.shape[-1]
    hd = d // heads

    @pl.when(pl.program_id(1) == 0)
    def _():
        if len(k_ref.shape) == 4:
            for h in range(heads):
                kb_ref[:, h * hd:(h + 1) * hd] = k_ref[0, :, h, :].astype(BF16)
                vb_ref[:, h * hd:(h + 1) * hd] = v_ref[0, :, h, :].astype(BF16)
        else:
            kb_ref[...] = k_ref[0].astype(BF16)
            vb_ref[...] = v_ref[0].astype(BF16)

    scale = 1.0 / math.sqrt(hd)
    for h in range(heads):
        sl = slice(h * hd, (h + 1) * hd)
        s = _dot_nt(q_ref[0, :, sl].astype(BF16), kb_ref[:, sl]) * scale
        m = jnp.max(s, axis=-1, keepdims=True)
        pexp = jnp.exp(s - m)
        prob = pexp / jnp.sum(pexp, axis=-1, keepdims=True)
        o_ref[0, :, sl] = _dot(prob.astype(BF16), vb_ref[:, sl])


def _attention(q, k, v, heads):
    b, l, d = q.shape
    m = k.shape[1]
    tq = _tile(l, 512, SUBLANES)
    kv_block = (1,) + k.shape[1:]
    kv_spec = pl.BlockSpec(kv_block, lambda i, j: (i,) + (0,) * (len(kv_block) - 1))
    return pl.pallas_call(
        functools.partial(_attn_kernel, heads=heads),
        grid=(b, l // tq),
        in_specs=[pl.BlockSpec((1, tq, d), lambda i, j: (i, j, 0)), kv_spec, kv_spec],
        out_specs=pl.BlockSpec((1, tq, d), lambda i, j: (i, j, 0)),
        out_shape=jax.ShapeDtypeStruct((b, l, d), F32),
        scratch_shapes=[pltpu.VMEM((m, d), BF16), pltpu.VMEM((m, d), BF16)],
        compiler_params=_cparams(("parallel", "arbitrary")),
        name="mem_attn",
    )(q, k, v)


def _top_extract(s, iota_f, k):
    rank = jnp.full(s.shape, 255.0, F32)
    vals = []
    for r in range(k):
        m = jnp.max(s, axis=0, keepdims=True)
        idx = jnp.min(jnp.where(s == m, iota_f, 1e9), axis=0, keepdims=True)
        hit = iota_f == idx
        rank = jnp.where(hit, float(r), rank)
        s = jnp.where(hit, NEG_INF, s)
        vals.append(m)
    return rank, vals


def _bf16_words(x):
    bits = pltpu.bitcast(x.astype(BF16).astype(F32), jnp.uint32)
    return pltpu.bitcast(bits | (bits >> 16), jnp.int32)


def _peer_route_kernel(x_ref, g_ref, q_ref, sk_ref, xnt_ref, rk_ref, e2_ref, qe_ref, *, heads, k):
    x = x_ref[...]
    ms = jnp.mean(x * x, axis=-1, keepdims=True)
    xn = x * lax.rsqrt(ms + EPS) * g_ref[...]
    xnt_ref[...] = xn.T.astype(BF16)
    nk, half = sk_ref.shape[1], sk_ref.shape[2]
    tm = x.shape[0]
    iota_k = lax.broadcasted_iota(jnp.int32, (nk, tm), 0).astype(F32)

    kh = k // 2
    n_cand = k + (kh - 1) * kh + kh
    row_id = lax.broadcasted_iota(jnp.int32, (n_cand, tm), 0)
    pos = jnp.zeros((n_cand, tm), F32)
    pos = jnp.where(row_id < k, row_id.astype(F32), pos)
    for p_ in range(1, kh):
        lo = k + (p_ - 1) * kh
        pos = jnp.where((row_id >= lo) & (row_id < lo + kh), (row_id - lo + p_ * k).astype(F32), pos)
    lo = k + (kh - 1) * kh
    pos = jnp.where(row_id >= lo, ((row_id - lo + kh) * k).astype(F32), pos)

    def head_body(h, carry):
        s_half = []
        for half_i in range(2):
            col0 = pl.multiple_of((h * 2 + half_i) * half, half)
            s_half.append(_dot_nt(sk_ref[h * 2 + half_i], q_ref[:, pl.ds(col0, half)]))
        rank1, a = _top_extract(s_half[0], iota_k, k)
        rank2, b = _top_extract(s_half[1], iota_k, k)
        b_lo = jnp.concatenate(b[:kh], axis=0)
        cand = jnp.concatenate([a[0] + jnp.concatenate(b, axis=0)]
                               + [a[p_] + b_lo for p_ in range(1, kh)]
                               + [jnp.concatenate(a[kh:], axis=0) + b[0]], axis=0)
        c_max = a[0] + b[0]
        sel = jnp.zeros(cand.shape, F32)
        cur = cand
        for _ in range(k):
            m = jnp.max(cur, axis=0, keepdims=True)
            idx = jnp.min(jnp.where(cur == m, pos, 1e9), axis=0, keepdims=True)
            hit = pos == idx
            sel = jnp.where(hit, 1.0, sel)
            cur = jnp.where(hit, NEG_INF, cur)
        zsum = jnp.sum(sel * jnp.exp(cand - c_max), axis=0, keepdims=True)
        q_rows = [jnp.sum(sel[0:k], axis=0, keepdims=True)]
        for p_ in range(1, kh):
            lo_ = k + (p_ - 1) * kh
            q_rows.append(jnp.sum(sel[lo_:lo_ + kh], axis=0, keepdims=True))
        lo_ = k + (kh - 1) * kh
        for p_ in range(kh, k):
            q_rows.append(sel[lo_ + p_ - kh:lo_ + p_ - kh + 1])
        qmap = jnp.zeros((nk, tm), F32)
        for p_ in range(k):
            qmap = jnp.where(rank1 == float(p_), q_rows[p_], qmap)
        rk_ref[h] = rank2.astype(BF16)
        e2_ref[h] = jnp.exp(s_half[1] - b[0]).astype(BF16)
        qe_ref[0, h] = _bf16_words(qmap)
        qe_ref[1, h] = _bf16_words(jnp.exp(s_half[0] - a[0]) * (1.0 / zsum))
        return carry

    lax.fori_loop(0, heads, head_body, 0)


def _peer_route(x, g, q, sk):
    t, d = x.shape
    tm = LANES
    heads2, nk, half = sk.shape
    heads = heads2 // 2
    return pl.pallas_call(
        functools.partial(_peer_route_kernel, heads=heads, k=PEER_TOPK),
        grid=(t // tm,),
        in_specs=[pl.BlockSpec((tm, d), lambda i: (i, 0)),
                  pl.BlockSpec((1, d), lambda i: (0, 0)),
                  pl.BlockSpec((tm, q.shape[1]), lambda i: (i, 0)),
                  pl.BlockSpec(sk.shape, lambda i: (0, 0, 0))],
        out_specs=[pl.BlockSpec((d, tm), lambda i: (0, i)),
                   pl.BlockSpec((heads, nk, tm), lambda i: (0, 0, i)),
                   pl.BlockSpec((heads, nk, tm), lambda i: (0, 0, i)),
                   pl.BlockSpec((2, heads, nk, tm), lambda i: (0, 0, 0, i))],
        out_shape=[jax.ShapeDtypeStruct((d, t), BF16), jax.ShapeDtypeStruct((heads, nk, t), BF16),
                   jax.ShapeDtypeStruct((heads, nk, t), BF16), jax.ShapeDtypeStruct((2, heads, nk, t), jnp.int32)],
        compiler_params=_cparams(("parallel",)),
        name="peer_route",
    )(x, g.reshape(1, d), q, sk)


def _peer_main_kernel(xnt_ref, rk_ref, e2_ref, qe_ref, u_ref, vt_ref, o_ref, acc_ref, ht0, ht1, p0, p1,
                      *, heads, nk, n_tiles):
    s = pl.program_id(1)
    te, tm = ht0.shape
    ib = te // nk

    @pl.when(s == 0)
    def _():
        acc_ref[...] = jnp.zeros_like(acc_ref)
        ht1[...] = jnp.zeros_like(ht1)
        p0[...] = jnp.zeros_like(p0)

    def phase_a(slot, ht):
        ht[...] = _dot(u_ref[slot * te:(slot + 1) * te, :], xnt_ref[...])

    def phase_b(tile, valid, ht, p):
        first_key = jnp.clip(tile, 0, n_tiles - 1) * ib
        for ii in range(ib):
            i = first_key + ii
            act = _gelu_exact(ht[ii * nk:(ii + 1) * nk, :]).astype(BF16)
            gate = None
            for h in range(heads):
                q_w = jnp.broadcast_to(qe_ref[0, h, pl.ds(i, 1), :] * valid, (SUBLANES, tm))
                e_w = jnp.broadcast_to(qe_ref[1, h, pl.ds(i, 1), :], (SUBLANES, tm))
                q_b = pltpu.repeat(pltpu.bitcast(q_w, BF16), nk // (2 * SUBLANES), axis=0)
                e_b = pltpu.repeat(pltpu.bitcast(e_w, BF16), nk // (2 * SUBLANES), axis=0)
                t = jnp.where(rk_ref[h] < q_b, e2_ref[h], 0.0) * e_b
                gate = t if gate is None else gate + t
            p[ii * nk:(ii + 1) * nk, :] = gate * act

    def phase_c(slot, p):
        acc_ref[...] += _dot(vt_ref[:, slot * te:(slot + 1) * te], p[...])

    phase_a(0, ht0)
    phase_b(2 * s - 1, (s >= 1).astype(jnp.int32), ht1, p1)
    phase_c(0, p0)
    pl.delay(1)
    phase_a(1, ht1)
    phase_b(2 * s, (s < n_tiles // 2).astype(jnp.int32), ht0, p0)
    phase_c(1, p1)

    @pl.when(s == pl.num_programs(1) - 1)
    def _():
        o_ref[...] = acc_ref[...].T


def _peer_main(xnt, rk, e2, qe, u, vt, nk):
    d, t = xnt.shape
    n_exp = u.shape[0]
    heads = rk.shape[0]
    tm = _tile(t, 512, LANES)
    te = _tile(n_exp // 2, 512, nk)
    n_tiles = n_exp // te
    assert n_tiles % 2 == 0 and nk % (2 * SUBLANES) == 0
    n_pairs = n_tiles // 2
    return pl.pallas_call(
        functools.partial(_peer_main_kernel, heads=heads, nk=nk, n_tiles=n_tiles),
        grid=(t // tm, n_pairs + 1),
        in_specs=[pl.BlockSpec((d, tm), lambda i, s: (0, i)),
                  pl.BlockSpec((heads, nk, tm), lambda i, s: (0, 0, i)),
                  pl.BlockSpec((heads, nk, tm), lambda i, s: (0, 0, i)),
                  pl.BlockSpec((2, heads, nk, tm), lambda i, s: (0, 0, 0, i)),
                  pl.BlockSpec((2 * te, d), lambda i, s: (jnp.minimum(s, n_pairs - 1), 0)),
                  pl.BlockSpec((d, 2 * te), lambda i, s: (0, jnp.maximum(s - 1, 0)))],
        out_specs=pl.BlockSpec((tm, d), lambda i, s: (i, 0)),
        out_shape=jax.ShapeDtypeStruct((t, d), F32),
        scratch_shapes=[pltpu.VMEM((d, tm), F32), pltpu.VMEM((te, tm), F32), pltpu.VMEM((te, tm), F32),
                        pltpu.VMEM((te, tm), BF16), pltpu.VMEM((te, tm), BF16)],
        compiler_params=_cparams(("parallel", "arbitrary")),
        name="peer_main",
    )(xnt, rk, e2, qe, u, vt)


def _add_rms_kernel(x_ref, d_ref, g_ref, o_ref):
    x = x_ref[...] + d_ref[...]
    ms = jnp.mean(x * x, axis=-1, keepdims=True)
    o_ref[...] = x * lax.rsqrt(ms + EPS) * g_ref[...]


def _add_rms(x, delta, g):
    t, d = x.shape
    tm = _tile(t, 512, SUBLANES)
    return pl.pallas_call(
        _add_rms_kernel,
        grid=(t // tm,),
        in_specs=[pl.BlockSpec((tm, d), lambda i: (i, 0)), pl.BlockSpec((tm, d), lambda i: (i, 0)),
                  pl.BlockSpec((1, d), lambda i: (0, 0))],
        out_specs=pl.BlockSpec((tm, d), lambda i: (i, 0)),
        out_shape=jax.ShapeDtypeStruct((t, d), F32),
        compiler_params=_cparams(("parallel",)),
        name="add_rms",
    )(x, delta, g.reshape(1, d))


def _pad_lanes(v):
    return jnp.pad(v, (0, LANES - v.shape[0])).reshape(1, LANES)


def _layer_params(i, dims, norm_mix, w_in, ssd_conv_w, ssd_conv_b, ssd_dt_bias, ssd_a_log, ssd_d, ssd_norm,
                  sc_conv_w, w_out, norm_mem_q, norm_mem_kv, w_mem_q, w_mem_k, w_mem_v, w_mem_o, norm_ffn,
                  w_peer_q, peer_sub_keys, peer_u, peer_v):
    w, gn, hh, scw = dims["w"], dims["gn"], dims["heads"], dims["scw"]
    cd = w + 2 * gn
    s1, s2, s3 = w, w + cd, w + cd + hh
    s4, s5 = s3 + scw, s3 + 2 * scw
    wi = w_in[i]
    n_used = 5 * w + 2 * gn + hh
    n_pad = -(-n_used // IN_PROJ_COL_TILE) * IN_PROJ_COL_TILE - n_used
    w_in_p = jnp.concatenate(
        [wi[:, s1:s1 + w], wi[:, :s1], wi[:, s3:s4], wi[:, s4:s5], wi[:, s5:], wi[:, s1 + w:s1 + w + gn],
         wi[:, s1 + w + gn:s2], jnp.pad(wi[:, s2:s3], ((0, 0), (0, n_pad)))], axis=1).astype(BF16)
    off = {"xs": 0, "z": w, "sch": 2 * w, "scb": 3 * w, "scc": 4 * w, "bm": 5 * w, "cm": 5 * w + gn,
           "dt": 5 * w + 2 * gn}
    cw, cb = ssd_conv_w[i], ssd_conv_b[i]
    mix = dict(dims, off=off,
               cwx=cw[:, :w], cwb=cw[:, w:w + gn], cwc=cw[:, w + gn:],
               cbx=cb[:w].reshape(1, w), cbb=cb[w:w + gn].reshape(1, gn), cbc=cb[w + gn:].reshape(1, gn),
               dtb=_pad_lanes(ssd_dt_bias[i]), alog=_pad_lanes(ssd_a_log[i]),
               dx=jnp.repeat(ssd_d[i], dims["p"]).reshape(1, w), nw=ssd_norm[i].reshape(1, w), scw=sc_conv_w[i])
    ph, _, nk, half = peer_sub_keys[i].shape
    return dict(
        norm_mix=norm_mix[i], w_in=w_in_p, mix=mix,
        w_out_ssd=w_out[i][:w].astype(BF16), w_out_sc=w_out[i][w:].astype(BF16),
        norm_mem_q=norm_mem_q[i], norm_mem_kv=norm_mem_kv[i], w_mem_q=w_mem_q[i].astype(BF16),
        w_mem_kv=jnp.concatenate([w_mem_k[i], w_mem_v[i]], axis=1).astype(BF16), w_mem_o=w_mem_o[i].astype(BF16),
        norm_ffn=norm_ffn[i], w_peer_q=w_peer_q[i].astype(BF16),
        sub_keys=peer_sub_keys[i].reshape(ph * 2, nk, half).astype(BF16),
        peer_u=peer_u[i].astype(BF16), peer_v_t=peer_v[i].T.astype(BF16), nk=nk)


def _attn_peer(x1, mem_k, mem_v, lp, seq_shape, mem_heads):
    b, l = seq_shape
    d = x1.shape[1]
    q = _rms_matmul(x1, lp["norm_mem_q"], lp["w_mem_q"])
    o = _attention(q.reshape(b, l, d), mem_k, mem_v, mem_heads).reshape(b * l, d)
    x2 = _matmul_res([o], [lp["w_mem_o"]], x1)
    pq = _rms_matmul(x2, lp["norm_ffn"], lp["w_peer_q"], out_dtype=BF16)
    xnt, rk, e2, qe = _peer_route(x2, lp["norm_ffn"], pq, lp["sub_keys"])
    delta = _peer_main(xnt, rk, e2, qe, lp["peer_u"], lp["peer_v_t"], lp["nk"])
    return x2, delta


def kernel(x_prompt, x_sample, mem_prompt, state_ssm, state_ssd_conv, state_short_conv, cache_mem_k, cache_mem_v,
           norm_mix, w_in, ssd_conv_w, ssd_conv_b, ssd_dt_bias, ssd_a_log, ssd_d, ssd_norm, sc_conv_w, w_out,
           norm_mem_q, norm_mem_kv, w_mem_q, w_mem_k, w_mem_v, w_mem_o, norm_ffn, w_peer_q, peer_sub_keys,
           peer_u, peer_v, norm_final):
    bp, lp_, d = x_prompt.shape
    bs, ls, _ = x_sample.shape
    depth, _, heads, p, n = state_ssm.shape
    k_ssd = state_ssd_conv.shape[2] + 1
    cd = state_ssd_conv.shape[3]
    k_sc = state_short_conv.shape[2] + 1
    scw = state_short_conv.shape[3]
    w = heads * p
    gn = (cd - w) // 2
    g = gn // n
    mem_tokens, mem_heads, mem_hd = cache_mem_k.shape[2:]
    assert scw == w and d == mem_heads * mem_hd
    dims = dict(w=w, gn=gn, n=n, p=p, g=g, heads=heads, scw=scw)
    lc_p = min(SSD_CHUNK, lp_)
    nb_s = 16 if (bs % 16 == 0) else SUBLANES
    assert lp_ % lc_p == 0 and ls <= SSD_CHUNK and bs % nb_s == 0

    hp = x_prompt.reshape(bp * lp_, d)
    hs = jnp.transpose(x_sample, (1, 0, 2)).reshape(ls * bs, d)
    outs = {k_: [] for k_ in ("p_ssm", "p_ssd", "p_sc", "p_mk", "p_mv", "s_ssm", "s_ssd", "s_sc")}
    for i in range(depth):
        lp = _layer_params(i, dims, norm_mix, w_in, ssd_conv_w, ssd_conv_b, ssd_dt_bias, ssd_a_log, ssd_d,
                           ssd_norm, sc_conv_w, w_out, norm_mem_q, norm_mem_kv, w_mem_q, w_mem_k, w_mem_v,
                           w_mem_o, norm_ffn, w_peer_q, peer_sub_keys, peer_u, peer_v)
        kv = _rms_matmul(mem_prompt.reshape(bp * mem_tokens, d), lp["norm_mem_kv"], lp["w_mem_kv"])
        mk = kv[:, :d].reshape(bp, mem_tokens, d)
        mv = kv[:, d:].reshape(bp, mem_tokens, d)
        h = _rms_matmul(hp, lp["norm_mix"], lp["w_in"]).reshape(bp, lp_, -1)
        y_ssd, y_sc, n_ssm, ntx, ntb, ntc, ntsc = _mixer(
            h, jnp.zeros((bp, w, n), F32), jnp.zeros((bp, k_ssd - 1, cd), F32), jnp.zeros((bp, k_sc - 1, scw), F32),
            lp["mix"], time_major=False, lc=lc_p, nb=1, gb=g)
        x1 = _matmul_res([y_ssd.reshape(bp * lp_, w), y_sc.reshape(bp * lp_, w)],
                         [lp["w_out_ssd"], lp["w_out_sc"]], hp)
        x2, delta = _attn_peer(x1, mk, mv, lp, (bp, lp_), mem_heads)
        if i == depth - 1:
            hp = _add_rms(x2, delta, norm_final)
        else:
            hp = x2 + delta
        outs["p_ssm"].append(n_ssm.reshape(bp, heads, p, n))
        outs["p_ssd"].append(jnp.concatenate([ntx, ntb, ntc], axis=-1))
        outs["p_sc"].append(ntsc)
        outs["p_mk"].append(mk.reshape(bp, mem_tokens, mem_heads, mem_hd))
        outs["p_mv"].append(mv.reshape(bp, mem_tokens, mem_heads, mem_hd))
        h = _rms_matmul(hs, lp["norm_mix"], lp["w_in"]).reshape(ls, bs, -1)
        y_ssd, y_sc, m_ssm, ntx, ntb, ntc, ntsc = _mixer(
            h, state_ssm[i].reshape(bs, w, n), jnp.transpose(state_ssd_conv[i], (1, 0, 2)),
            jnp.transpose(state_short_conv[i], (1, 0, 2)), lp["mix"], time_major=True, lc=ls, nb=nb_s, gb=1)
        x1 = _matmul_res([y_ssd.reshape(ls * bs, w), y_sc.reshape(ls * bs, w)],
                         [lp["w_out_ssd"], lp["w_out_sc"]], hs)
        x1 = jnp.transpose(x1.reshape(ls, bs, d), (1, 0, 2)).reshape(bs * ls, d)
        x2, delta = _attn_peer(x1, cache_mem_k[i], cache_mem_v[i], lp, (bs, ls), mem_heads)
        if i == depth - 1:
            hs_bm = _add_rms(x2, delta, norm_final)
        else:
            hs_bm = x2 + delta
        hs = jnp.transpose(hs_bm.reshape(bs, ls, d), (1, 0, 2)).reshape(ls * bs, d) if i < depth - 1 else hs_bm
        outs["s_ssm"].append(m_ssm.reshape(bs, heads, p, n))
        outs["s_ssd"].append(jnp.transpose(jnp.concatenate([ntx, ntb, ntc], axis=-1), (1, 0, 2)))
        outs["s_sc"].append(jnp.transpose(ntsc, (1, 0, 2)))
    y_prompt = hp.reshape(bp, lp_, d)
    y_sample = hs.reshape(bs, ls, d)
    return (y_prompt, y_sample, jnp.stack(outs["p_ssm"]), jnp.stack(outs["p_ssd"]), jnp.stack(outs["p_sc"]),
            jnp.stack(outs["p_mk"]), jnp.stack(outs["p_mv"]), jnp.stack(outs["s_ssm"]), jnp.stack(outs["s_ssd"]),
            jnp.stack(outs["s_sc"]))
```

```python
import functools
import math

import jax
import jax.numpy as jnp
import numpy as np
from jax import lax
from jax.experimental import pallas as pl
from jax.experimental.pallas import tpu as pltpu

F32 = jnp.float32
BF16 = jnp.bfloat16

EPS = 1e-6
SSD_CHUNK = 128
PEER_TOPK = 16
LANES = 128
SUBLANES = 8
VMEM_LIMIT_BYTES = 56 * 1024 * 1024
NEG_INF = float("-inf")
IN_PROJ_COL_TILE = 512
ROUTE_TOKENS = 256
PEER_GATE_LANES = 256
PEER_EXPERT_TILE = 1024


def _cparams(sem, flags=None):
    return pltpu.CompilerParams(dimension_semantics=sem, vmem_limit_bytes=VMEM_LIMIT_BYTES, flags=flags)


def _tile(n, pref, mult):
    if n <= pref:
        return n
    t = (pref // mult) * mult
    while t >= mult:
        if n % t == 0:
            return t
        t -= mult
    return n


def _hdot(a, b):
    return jnp.dot(a, b, precision=lax.Precision.HIGHEST, preferred_element_type=F32)


def _hdot_nt(a, b):
    return lax.dot_general(a, b, (((1,), (1,)), ((), ())), precision=lax.Precision.HIGHEST,
                           preferred_element_type=F32)


def _dot(a, b):
    return jnp.dot(a, b, preferred_element_type=F32)


def _dot_nt(a, b):
    return lax.dot_general(a, b, (((1,), (1,)), ((), ())), preferred_element_type=F32)


def _silu(x):
    return x * (1.0 / (1.0 + jnp.exp(-x)))


def _softplus(x):
    return jnp.maximum(x, 0.0) + jnp.log1p(jnp.exp(-jnp.abs(x)))


def _gelu_exact(x):
    return 0.5 * x * (1.0 + lax.erf(x * (1.0 / math.sqrt(2.0))))


def _rms_matmul_kernel(x_ref, g_ref, w_ref, o_ref, xn_ref):
    @pl.when(pl.program_id(1) == 0)
    def _():
        x = x_ref[...]
        ms = jnp.mean(x * x, axis=-1, keepdims=True)
        xn_ref[...] = (x * lax.rsqrt(ms + EPS) * g_ref[...]).astype(BF16)

    o_ref[...] = _dot(xn_ref[...], w_ref[...]).astype(o_ref.dtype)


def _rms_matmul(x, g, w, out_dtype=F32):
    t, d = x.shape
    n = w.shape[1]
    tm = _tile(t, 1024, SUBLANES)
    tn = 1024 if n % 1024 == 0 else _tile(n, 512, LANES)
    return pl.pallas_call(
        _rms_matmul_kernel,
        grid=(t // tm, n // tn),
        in_specs=[pl.BlockSpec((tm, d), lambda i, j: (i, 0)),
                  pl.BlockSpec((1, d), lambda i, j: (0, 0)),
                  pl.BlockSpec((d, tn), lambda i, j: (0, j))],
        out_specs=pl.BlockSpec((tm, tn), lambda i, j: (i, j)),
        out_shape=jax.ShapeDtypeStruct((t, n), out_dtype),
        scratch_shapes=[pltpu.VMEM((tm, d), BF16)],
        compiler_params=_cparams(("parallel", "arbitrary")),
        name="rms_matmul",
    )(x, g.reshape(1, d), w)


def _matmul_res_kernel(*refs, n_a):
    a_refs = refs[:n_a]
    w_refs = refs[n_a:2 * n_a]
    r_ref, o_ref = refs[2 * n_a], refs[2 * n_a + 1]
    ab_refs = refs[2 * n_a + 2:]

    @pl.when(pl.program_id(1) == 0)
    def _():
        for a_ref, ab_ref in zip(a_refs, ab_refs):
            ab_ref[...] = a_ref[...].astype(BF16)

    acc = r_ref[...]
    for ab_ref, w_ref in zip(ab_refs, w_refs):
        acc = acc + _dot(ab_ref[...], w_ref[...])
    o_ref[...] = acc


def _matmul_res(a_list, w_list, res):
    t, d = res.shape
    tm = _tile(t, 1024, SUBLANES)
    tn = _tile(d, 512, LANES)
    n_a = len(a_list)
    in_specs = ([pl.BlockSpec((tm, a.shape[1]), lambda i, j: (i, 0)) for a in a_list]
                + [pl.BlockSpec((w.shape[0], tn), lambda i, j: (0, j)) for w in w_list]
                + [pl.BlockSpec((tm, tn), lambda i, j: (i, j))])
    return pl.pallas_call(
        functools.partial(_matmul_res_kernel, n_a=n_a),
        grid=(t // tm, d // tn),
        in_specs=in_specs,
        out_specs=pl.BlockSpec((tm, tn), lambda i, j: (i, j)),
        out_shape=jax.ShapeDtypeStruct((t, d), F32),
        scratch_shapes=[pltpu.VMEM((tm, a.shape[1]), BF16) for a in a_list],
        compiler_params=_cparams(("parallel", "arbitrary")),
        name="matmul_res",
    )(*a_list, *w_list, res)


def _mixer_kernel(xs_ref, z_ref, sch_ref, scb_ref, scc_ref, bm_ref, cm_ref, dt_ref,
                  s0_ref, tx_ref, tb_ref, tc_ref, tsc_ref,
                  cwx_ref, cwb_ref, cwc_ref, cbx_ref, cbb_ref, cbc_ref,
                  dtb_ref, alog_ref, sel_ref, dx_ref, nw_ref, scw_ref,
                  e_ref, et_ref, tri_ref, blk_ref, eye_ref,
                  yssd_ref, ysc_ref, sout_ref, ntx_ref, ntb_ref, ntc_ref, ntsc_ref,
                  s_scr, padx, padb, padc, padu,
                  *, lc, nb, gb, rh, p, n, n_chunks, time_major):
    rows = lc * nb
    gw = rh * p
    hr = padx.shape[0] - rows
    k_ssd = cwx_ref.shape[0]
    k_sc = scw_ref.shape[0]
    c = pl.program_id(2)

    def ld(ref):
        if time_major:
            return ref[...].reshape(ref.shape[0] * ref.shape[1], ref.shape[2])
        return ref[0]

    def st(ref, val):
        if time_major:
            ref[...] = val.reshape(ref.shape)
        else:
            ref[0] = val

    @pl.when(c == 0)
    def _():
        s_scr[...] = s0_ref[...]
        padx[hr - (k_ssd - 1) * nb:hr, :] = ld(tx_ref)
        padb[hr - (k_ssd - 1) * nb:hr, :] = ld(tb_ref)
        padc[hr - (k_ssd - 1) * nb:hr, :] = ld(tc_ref)
        padu[hr - (k_sc - 1) * nb:hr, :] = ld(tsc_ref)

    def conv(pad, x, w_ref, nk):
        pad[hr:hr + rows, :] = x
        acc = None
        for j in range(nk):
            off = hr - (nk - 1 - j) * nb
            term = pad[off:off + rows, :] * w_ref[j:j + 1, :]
            acc = term if acc is None else acc + term
        tail = pad[hr + rows - (nk - 1) * nb:hr + rows, :]
        if n_chunks > 1:
            pad[hr - (nk - 1) * nb:hr, :] = tail
        return acc, tail

    xs_pre, tail_x = conv(padx, ld(xs_ref), cwx_ref, k_ssd)
    b_pre, tail_b = conv(padb, ld(bm_ref), cwb_ref, k_ssd)
    c_pre, tail_c = conv(padc, ld(cm_ref), cwc_ref, k_ssd)
    cv, tail_u = conv(padu, ld(scc_ref) * ld(sch_ref), scw_ref, k_sc)
    st(ntx_ref, tail_x)
    st(ntb_ref, tail_b)
    st(ntc_ref, tail_c)
    st(ntsc_ref, tail_u)
    st(ysc_ref, ld(scb_ref) * cv)

    xs_c = _silu(xs_pre + cbx_ref[...])
    b_c = _silu(b_pre + cbb_ref[...])
    c_c = _silu(c_pre + cbc_ref[...])

    dt_all = _softplus(ld(dt_ref) + dtb_ref[...])
    a_all = dt_all * (-jnp.exp(alog_ref[...]))
    sel = sel_ref[0]
    dt = _hdot(dt_all, sel)
    a = _hdot(a_all, sel)
    tri = tri_ref[...]
    a_cs = _hdot(tri, a)
    a_tot = _hdot(blk_ref[...], a)
    e = e_ref[...]
    dt_x = _hdot(dt, e)
    wst_x = _hdot(jnp.exp(a_tot - a_cs), e)
    woff_x = _hdot(jnp.exp(a_cs), e)
    xdt = xs_c * dt_x
    xw = xdt * wst_x
    eye = eye_ref[...]
    a_cs_t = _hdot_nt(eye, a_cs)
    tri_mask = tri > 0.5
    lane_id = lax.broadcasted_iota(jnp.int32, (rows, LANES), 1)
    row_seq_g = lax.broadcasted_iota(jnp.int32, (rows, gw), 0) % nb
    row_seq_n = lax.broadcasted_iota(jnp.int32, (rows, n), 0) % nb

    dec_cols = _hdot(et_ref[...], jnp.exp(_hdot_nt(eye, a_tot)))
    decs = [jnp.broadcast_to(dec_cols[:, b:b + 1], (gb * gw, n)) for b in range(nb)]

    z = ld(z_ref)
    y_cols = []
    for gl in range(gb):
        cg = c_c[:, gl * n:(gl + 1) * n].astype(BF16)
        bg = b_c[:, gl * n:(gl + 1) * n]
        cb = _dot_nt(cg, bg.astype(BF16))
        ydiag = []
        for pr in range(gw // LANES):
            xpair = xdt[:, gl * gw + pr * LANES:gl * gw + (pr + 1) * LANES]
            acc = None
            for sub in range(LANES // p):
                hl = gl * rh + pr * (LANES // p) + sub
                col = a_cs[:, hl:hl + 1]
                row = a_cs_t[hl:hl + 1, :]
                lmat = jnp.exp(jnp.where(tri_mask, col - row, NEG_INF))
                m = (cb * lmat).astype(BF16)
                xm = jnp.where((lane_id >= sub * p) & (lane_id < (sub + 1) * p), xpair, 0.0).astype(BF16)
                t = _dot(m, xm)
                acc = t if acc is None else acc + t
            ydiag.append(acc)
        ydiag = ydiag[0] if len(ydiag) == 1 else jnp.concatenate(ydiag, axis=1)

        xw_t = xw[:, gl * gw:(gl + 1) * gw].T.astype(BF16)
        yoff = None
        for b in range(nb):
            s_b = s_scr[b, gl * gw:(gl + 1) * gw, :]
            y_b = _dot_nt(cg, s_b.astype(BF16))
            if nb > 1:
                y_b = jnp.where(row_seq_g == b, y_b, 0.0)
                bg_b = jnp.where(row_seq_n == b, bg, 0.0).astype(BF16)
            else:
                bg_b = bg.astype(BF16)
            yoff = y_b if yoff is None else yoff + y_b
            s_scr[b, gl * gw:(gl + 1) * gw, :] = decs[b][gl * gw:(gl + 1) * gw, :] * s_b + _dot(xw_t, bg_b)

        sl = slice(gl * gw, (gl + 1) * gw)
        y = ydiag + yoff * woff_x[:, sl] + dx_ref[:, sl] * xs_c[:, sl]
        yg = y * _silu(z[:, sl])
        ms = jnp.mean(yg * yg, axis=-1, keepdims=True)
        y_cols.append(yg * lax.rsqrt(ms + EPS) * nw_ref[:, sl])
    y_all = y_cols[0] if gb == 1 else jnp.concatenate(y_cols, axis=1)
    st(yssd_ref, y_all)

    @pl.when(c == n_chunks - 1)
    def _():
        sout_ref[...] = s_scr[...]


def _mixer(h, s0, tail_ssd, tail_sc, prm, *, time_major, lc, nb, gb):
    w, gn, n, p, g = prm["w"], prm["gn"], prm["n"], prm["p"], prm["g"]
    rh = (w // p) // g
    gw = rh * p
    bw = gb * gw
    bn = gb * n
    if time_major:
        seq_len, n_seq = h.shape[0], h.shape[1]
        n_chunks = 1
        assert seq_len == lc
    else:
        n_seq, seq_len = h.shape[0], h.shape[1]
        n_chunks = seq_len // lc
        assert nb == 1 and seq_len % lc == 0
    rows = lc * nb
    k_ssd = prm["cwx"].shape[0]
    k_sc = prm["scw"].shape[0]
    assert lc >= k_ssd - 1 and n_seq % nb == 0 and g % gb == 0
    assert w % bw == 0 and w % bn == 0 and (w + gn) % bn == 0 and LANES % p == 0 and gw % LANES == 0
    hr = -(-((k_ssd - 1) * nb) // SUBLANES) * SUBLANES
    off = prm["off"]

    def hspec(width, col0):
        assert col0 % width == 0
        cb0 = col0 // width
        if time_major:
            return pl.BlockSpec((lc, nb, width), lambda i, j, c: (0, i, cb0 + j))
        return pl.BlockSpec((1, lc, width), lambda i, j, c: (i, c, cb0 + j))

    def hspec_fixed(width, col0):
        cb0 = col0 // width
        if time_major:
            return pl.BlockSpec((lc, nb, width), lambda i, j, c: (0, i, cb0))
        return pl.BlockSpec((1, lc, width), lambda i, j, c: (i, c, cb0))

    def tspec(k, width, col0):
        assert col0 % width == 0
        cb0 = col0 // width
        if time_major:
            return pl.BlockSpec((k, nb, width), lambda i, j, c: (0, i, cb0 + j))
        return pl.BlockSpec((1, k, width), lambda i, j, c: (i, 0, cb0 + j))

    def ospec(width):
        if time_major:
            return pl.BlockSpec((lc, nb, width), lambda i, j, c: (0, i, j))
        return pl.BlockSpec((1, lc, width), lambda i, j, c: (i, c, j))

    def otspec(k, width):
        if time_major:
            return pl.BlockSpec((k, nb, width), lambda i, j, c: (0, i, j))
        return pl.BlockSpec((1, k, width), lambda i, j, c: (i, 0, j))

    def pspec(rows_, width):
        return pl.BlockSpec((rows_, width), lambda i, j, c: (0, j))

    def cspec(shape):
        return pl.BlockSpec(shape, lambda i, j, c: (0,) * len(shape))

    def oshape(k, width):
        return (k, n_seq, width) if time_major else (n_seq, k, width)

    r = np.arange(rows)
    same = (r[:, None] % nb) == (r[None, :] % nb)
    tri = (same & ((r[None, :] // nb) <= (r[:, None] // nb))).astype(np.float32)
    blk = same.astype(np.float32)
    hl = gb * rh
    e_mat = np.zeros((LANES, bw), np.float32)
    e_mat[np.arange(bw) // p, np.arange(bw)] = 1.0
    sel = np.zeros((g // gb, LANES, LANES), np.float32)
    for j in range(g // gb):
        sel[j, j * hl + np.arange(hl), np.arange(hl)] = 1.0

    in_specs = [
        hspec(bw, off["xs"]), hspec(bw, off["z"]), hspec(bw, off["sch"]), hspec(bw, off["scb"]),
        hspec(bw, off["scc"]), hspec(bn, off["bm"]), hspec(bn, off["cm"]), hspec_fixed(LANES, off["dt"]),
        pl.BlockSpec((nb, bw, n), lambda i, j, c: (i, j, 0)),
        tspec(k_ssd - 1, bw, 0), tspec(k_ssd - 1, bn, w), tspec(k_ssd - 1, bn, w + gn), tspec(k_sc - 1, bw, 0),
        pspec(k_ssd, bw), pspec(k_ssd, bn), pspec(k_ssd, bn), pspec(1, bw), pspec(1, bn), pspec(1, bn),
        cspec((1, LANES)), cspec((1, LANES)),
        pl.BlockSpec((1, LANES, LANES), lambda i, j, c: (j, 0, 0)),
        pspec(1, bw), pspec(1, bw), pspec(k_sc, bw),
        cspec((LANES, bw)), cspec((bw, LANES)), cspec((rows, rows)), cspec((rows, rows)), cspec((LANES, LANES)),
    ]
    out_specs = [
        ospec(bw), ospec(bw),
        pl.BlockSpec((nb, bw, n), lambda i, j, c: (i, j, 0)),
        otspec(k_ssd - 1, bw), otspec(k_ssd - 1, bn), otspec(k_ssd - 1, bn), otspec(k_sc - 1, bw),
    ]
    out_shape = [
        jax.ShapeDtypeStruct(h.shape[:2] + (w,), F32), jax.ShapeDtypeStruct(h.shape[:2] + (w,), F32),
        jax.ShapeDtypeStruct((n_seq, w, n), F32),
        jax.ShapeDtypeStruct(oshape(k_ssd - 1, w), F32), jax.ShapeDtypeStruct(oshape(k_ssd - 1, gn), F32),
        jax.ShapeDtypeStruct(oshape(k_ssd - 1, gn), F32), jax.ShapeDtypeStruct(oshape(k_sc - 1, w), F32),
    ]
    scratch = [pltpu.VMEM((nb, bw, n), F32), pltpu.VMEM((hr + rows, bw), F32), pltpu.VMEM((hr + rows, bn), F32),
               pltpu.VMEM((hr + rows, bn), F32), pltpu.VMEM((hr + rows, bw), F32)]
    kern = functools.partial(_mixer_kernel, lc=lc, nb=nb, gb=gb, rh=rh, p=p, n=n, n_chunks=n_chunks,
                             time_major=time_major)
    return pl.pallas_call(
        kern,
        grid=(n_seq // nb, g // gb, n_chunks),
        in_specs=in_specs, out_specs=out_specs, out_shape=out_shape, scratch_shapes=scratch,
        compiler_params=_cparams(("parallel", "parallel", "arbitrary")),
        name="mixer_tm" if time_major else "mixer",
    )(h, h, h, h, h, h, h, h, s0, tail_ssd, tail_ssd, tail_ssd, tail_sc,
      prm["cwx"], prm["cwb"], prm["cwc"], prm["cbx"], prm["cbb"], prm["cbc"],
      prm["dtb"], prm["alog"], jnp.asarray(sel), prm["dx"], prm["nw"], prm["scw"],
      jnp.asarray(e_mat), jnp.asarray(e_mat.T), jnp.asarray(tri), jnp.asarray(blk),
      jnp.asarray(np.eye(LANES, dtype=np.float32)))


def _attn_kernel(q_ref, k_ref, v_ref, o_ref, kb_ref, vb_ref, *, heads):
    d = q_ref.shape[-1]
    hd = d // heads

    @pl.when(pl.program_id(1) == 0)
    def _():
        if len(k_ref.shape) == 4:
            for h in range(heads):
                kb_ref[:, h * hd:(h + 1) * hd] = k_ref[0, :, h, :].astype(BF16)
                vb_ref[:, h * hd:(h + 1) * hd] = v_ref[0, :, h, :].astype(BF16)
        else:
            kb_ref[...] = k_ref[0].astype(BF16)
            vb_ref[...] = v_ref[0].astype(BF16)

    scale = 1.0 / math.sqrt(hd)
    for h in range(heads):
        sl = slice(h * hd, (h + 1) * hd)
        s = _dot_nt(q_ref[0, :, sl].astype(BF16), kb_ref[:, sl]) * scale
        m = jnp.max(s, axis=-1, keepdims=True)
        pexp = jnp.exp(s - m)
        prob = pexp / jnp.sum(pexp, axis=-1, keepdims=True)
        o_ref[0, :, sl] = _dot(prob.astype(BF16), vb_ref[:, sl])


def _attention(q, k, v, heads):
    b, l, d = q.shape
    m = k.shape[1]
    tq = _tile(l, 512, SUBLANES)
    kv_block = (1,) + k.shape[1:]
    kv_spec = pl.BlockSpec(kv_block, lambda i, j: (i,) + (0,) * (len(kv_block) - 1))
    return pl.pallas_call(
        functools.partial(_attn_kernel, heads=heads),
        grid=(b, l // tq),
        in_specs=[pl.BlockSpec((1, tq, d), lambda i, j: (i, j, 0)), kv_spec, kv_spec],
        out_specs=pl.BlockSpec((1, tq, d), lambda i, j: (i, j, 0)),
        out_shape=jax.ShapeDtypeStruct((b, l, d), F32),
        scratch_shapes=[pltpu.VMEM((m, d), BF16), pltpu.VMEM((m, d), BF16)],
        compiler_params=_cparams(("parallel", "arbitrary")),
        name="mem_attn",
    )(q, k, v)


def _top_extract(s, iota_f, k):
    rank = jnp.full(s.shape, 255.0, F32)
    vals = []
    for r in range(k):
        m = jnp.max(s, axis=0, keepdims=True)
        idx = jnp.min(jnp.where(s == m, iota_f, 1e9), axis=0, keepdims=True)
        hit = iota_f == idx
        rank = jnp.where(hit, float(r), rank)
        s = jnp.where(hit, NEG_INF, s)
        vals.append(m)
    return rank, vals


def _bf16_words(x):
    bits = pltpu.bitcast(x.astype(BF16).astype(F32), jnp.uint32)
    return pltpu.bitcast(bits | (bits >> 16), jnp.int32)


def _peer_route_kernel(x_ref, g_ref, q_ref, sk_ref, xnt_ref, rk_ref, e2_ref, qe_ref, *, heads, k):
    x = x_ref[...]
    ms = jnp.mean(x * x, axis=-1, keepdims=True)
    xn = x * lax.rsqrt(ms + EPS) * g_ref[...]
    xnt_ref[...] = xn.T.astype(BF16)
    nk, half = sk_ref.shape[1], sk_ref.shape[2]
    tm = x.shape[0]
    iota_k = lax.broadcasted_iota(jnp.int32, (nk, tm), 0).astype(F32)

    kh = k // 2
    n_cand = k + (kh - 1) * kh + kh
    row_id = lax.broadcasted_iota(jnp.int32, (n_cand, tm), 0)
    pos = jnp.zeros((n_cand, tm), F32)
    pos = jnp.where(row_id < k, row_id.astype(F32), pos)
    for p_ in range(1, kh):
        lo = k + (p_ - 1) * kh
        pos = jnp.where((row_id >= lo) & (row_id < lo + kh), (row_id - lo + p_ * k).astype(F32), pos)
    lo = k + (kh - 1) * kh
    pos = jnp.where(row_id >= lo, ((row_id - lo + kh) * k).astype(F32), pos)

    def head_body(h, carry):
        s_half = []
        for half_i in range(2):
            col0 = pl.multiple_of((h * 2 + half_i) * half, half)
            s_half.append(_dot_nt(sk_ref[h * 2 + half_i], q_ref[:, pl.ds(col0, half)]))
        rank1, a = _top_extract(s_half[0], iota_k, k)
        rank2, b = _top_extract(s_half[1], iota_k, k)
        b_lo = jnp.concatenate(b[:kh], axis=0)
        cand = jnp.concatenate([a[0] + jnp.concatenate(b, axis=0)]
                               + [a[p_] + b_lo for p_ in range(1, kh)]
                               + [jnp.concatenate(a[kh:], axis=0) + b[0]], axis=0)
        c_max = a[0] + b[0]
        sel = jnp.zeros(cand.shape, F32)
        cur = cand
        for _ in range(k):
            m = jnp.max(cur, axis=0, keepdims=True)
            idx = jnp.min(jnp.where(cur == m, pos, 1e9), axis=0, keepdims=True)
            hit = pos == idx
            sel = jnp.where(hit, 1.0, sel)
            cur = jnp.where(hit, NEG_INF, cur)
        zsum = jnp.sum(sel * jnp.exp(cand - c_max), axis=0, keepdims=True)
        q_rows = [jnp.sum(sel[0:k], axis=0, keepdims=True)]
        for p_ in range(1, kh):
            lo_ = k + (p_ - 1) * kh
            q_rows.append(jnp.sum(sel[lo_:lo_ + kh], axis=0, keepdims=True))
        lo_ = k + (kh - 1) * kh
        for p_ in range(kh, k):
            q_rows.append(sel[lo_ + p_ - kh:lo_ + p_ - kh + 1])
        qmap = jnp.zeros((nk, tm), F32)
        for p_ in range(k):
            qmap = jnp.where(rank1 == float(p_), q_rows[p_], qmap)
        rk_ref[h] = rank2.astype(BF16)
        e2_ref[h] = jnp.exp(s_half[1] - b[0]).astype(BF16)
        qe_ref[0, h] = _bf16_words(qmap)
        qe_ref[1, h] = _bf16_words(jnp.exp(s_half[0] - a[0]) * (1.0 / zsum))
        return carry

    lax.fori_loop(0, heads, head_body, 0)


def _peer_route(x, g, q, sk):
    t, d = x.shape
    tm = ROUTE_TOKENS if t % ROUTE_TOKENS == 0 else LANES
    heads2, nk, half = sk.shape
    heads = heads2 // 2
    return pl.pallas_call(
        functools.partial(_peer_route_kernel, heads=heads, k=PEER_TOPK),
        grid=(t // tm,),
        in_specs=[pl.BlockSpec((tm, d), lambda i: (i, 0)),
                  pl.BlockSpec((1, d), lambda i: (0, 0)),
                  pl.BlockSpec((tm, q.shape[1]), lambda i: (i, 0)),
                  pl.BlockSpec(sk.shape, lambda i: (0, 0, 0))],
        out_specs=[pl.BlockSpec((d, tm), lambda i: (0, i)),
                   pl.BlockSpec((heads, nk, tm), lambda i: (0, 0, i)),
                   pl.BlockSpec((heads, nk, tm), lambda i: (0, 0, i)),
                   pl.BlockSpec((2, heads, nk, tm), lambda i: (0, 0, 0, i))],
        out_shape=[jax.ShapeDtypeStruct((d, t), BF16), jax.ShapeDtypeStruct((heads, nk, t), BF16),
                   jax.ShapeDtypeStruct((heads, nk, t), BF16), jax.ShapeDtypeStruct((2, heads, nk, t), jnp.int32)],
        compiler_params=_cparams(("parallel",)),
        name="peer_route",
    )(x, g.reshape(1, d), q, sk)


def _peer_main_kernel(xnt_ref, rk_ref, e2_ref, qe_ref, u_ref, vt_ref, o_ref, qrow, *bufs, heads, nk):
    n_ch = len(bufs) // 3
    accs, hts, ps = bufs[:n_ch], bufs[n_ch:2 * n_ch], bufs[2 * n_ch:]
    s = pl.program_id(1)
    te, tc = hts[0].shape
    ib = te // nk

    @pl.when(s == 0)
    def _():
        for acc in accs:
            acc[...] = jnp.zeros_like(acc)

    for ii in range(ib):
        for h in range(heads):
            for kind in range(2):
                r = (ii * heads + h) * 2 + kind
                qrow[r:r + 1, :] = qe_ref[kind, h, pl.ds(s * ib + ii, 1), :]

    def scores(c):
        hts[c][...] = _dot(u_ref[...], xnt_ref[:, c * tc:(c + 1) * tc])

    def accumulate(c):
        accs[c][...] += _dot(vt_ref[0], ps[c][...])

    scores(0)
    for c, (ht, p) in enumerate(zip(hts, ps)):
        cols = slice(c * tc, (c + 1) * tc)
        for ii in range(ib):
            rows = slice(ii * nk, (ii + 1) * nk)
            act = _gelu_exact(ht[rows, :]).astype(BF16)
            gate = None
            for h in range(heads):
                r = (ii * heads + h) * 2
                q_w = jnp.broadcast_to(qrow[r:r + 1, cols], (SUBLANES, tc))
                e_w = jnp.broadcast_to(qrow[r + 1:r + 2, cols], (SUBLANES, tc))
                q_b = jnp.tile(pltpu.bitcast(q_w, BF16), (nk // (2 * SUBLANES), 1))
                e_b = jnp.tile(pltpu.bitcast(e_w, BF16), (nk // (2 * SUBLANES), 1))
                t = jnp.where(rk_ref[h, :, cols] < q_b, e2_ref[h, :, cols], 0.0) * e_b
                gate = t if gate is None else gate + t
            p[rows, :] = gate * act
        if c + 1 < n_ch:
            scores(c + 1)
        if c >= 1:
            accumulate(c - 1)
    accumulate(n_ch - 1)

    @pl.when(s == pl.num_programs(1) - 1)
    def _():
        for c, acc in enumerate(accs):
            o_ref[c * tc:(c + 1) * tc, :] = acc[...].T


def _peer_expert_tile(n_exp, nk):
    return _tile(n_exp, PEER_EXPERT_TILE, nk)


def _peer_v_slabs(v, nk):
    n_exp, d = v.shape
    te = _peer_expert_tile(n_exp, nk)
    return jnp.transpose(v.reshape(n_exp // te, te, d), (0, 2, 1)).astype(BF16)


def _peer_main(xnt, rk, e2, qe, u, vt, nk):
    d, t = xnt.shape
    n_exp = u.shape[0]
    heads = rk.shape[0]
    tm = _tile(t, 512, LANES)
    te = _peer_expert_tile(n_exp, nk)
    tc = min(tm, PEER_GATE_LANES)
    n_ch = tm // tc
    assert nk % (2 * SUBLANES) == 0 and vt.shape == (n_exp // te, d, te) and tm % tc == 0
    return pl.pallas_call(
        functools.partial(_peer_main_kernel, heads=heads, nk=nk),
        grid=(t // tm, n_exp // te),
        in_specs=[pl.BlockSpec((d, tm), lambda i, s: (0, i)),
                  pl.BlockSpec((heads, nk, tm), lambda i, s: (0, 0, i)),
                  pl.BlockSpec((heads, nk, tm), lambda i, s: (0, 0, i)),
                  pl.BlockSpec((2, heads, nk, tm), lambda i, s: (0, 0, 0, i)),
                  pl.BlockSpec((te, d), lambda i, s: (s, 0)),
                  pl.BlockSpec((1, d, te), lambda i, s: (s, 0, 0))],
        out_specs=pl.BlockSpec((tm, d), lambda i, s: (i, 0)),
        out_shape=jax.ShapeDtypeStruct((t, d), F32),
        scratch_shapes=([pltpu.VMEM((2 * heads * (te // nk), tm), jnp.int32)]
                        + [pltpu.VMEM((d, tc), F32)] * n_ch + [pltpu.VMEM((te, tc), F32)] * n_ch
                        + [pltpu.VMEM((te, tc), BF16)] * n_ch),
        compiler_params=_cparams(("parallel", "arbitrary")),
        name="peer_main",
    )(xnt, rk, e2, qe, u, vt)


def _add_rms_kernel(x_ref, d_ref, g_ref, o_ref):
    x = x_ref[...] + d_ref[...]
    ms = jnp.mean(x * x, axis=-1, keepdims=True)
    o_ref[...] = x * lax.rsqrt(ms + EPS) * g_ref[...]


def _add_rms(x, delta, g):
    t, d = x.shape
    tm = _tile(t, 512, SUBLANES)
    return pl.pallas_call(
        _add_rms_kernel,
        grid=(t // tm,),
        in_specs=[pl.BlockSpec((tm, d), lambda i: (i, 0)), pl.BlockSpec((tm, d), lambda i: (i, 0)),
                  pl.BlockSpec((1, d), lambda i: (0, 0))],
        out_specs=pl.BlockSpec((tm, d), lambda i: (i, 0)),
        out_shape=jax.ShapeDtypeStruct((t, d), F32),
        compiler_params=_cparams(("parallel",)),
        name="add_rms",
    )(x, delta, g.reshape(1, d))


def _pad_lanes(v):
    return jnp.pad(v, (0, LANES - v.shape[0])).reshape(1, LANES)


def _layer_params(i, dims, norm_mix, w_in, ssd_conv_w, ssd_conv_b, ssd_dt_bias, ssd_a_log, ssd_d, ssd_norm,
                  sc_conv_w, w_out, norm_mem_q, norm_mem_kv, w_mem_q, w_mem_k, w_mem_v, w_mem_o, norm_ffn,
                  w_peer_q, peer_sub_keys, peer_u, peer_v):
    w, gn, hh, scw = dims["w"], dims["gn"], dims["heads"], dims["scw"]
    cd = w + 2 * gn
    s1, s2, s3 = w, w + cd, w + cd + hh
    s4, s5 = s3 + scw, s3 + 2 * scw
    wi = w_in[i]
    n_used = 5 * w + 2 * gn + hh
    n_pad = -(-n_used // IN_PROJ_COL_TILE) * IN_PROJ_COL_TILE - n_used
    w_in_p = jnp.concatenate(
        [wi[:, s1:s1 + w], wi[:, :s1], wi[:, s3:s4], wi[:, s4:s5], wi[:, s5:], wi[:, s1 + w:s1 + w + gn],
         wi[:, s1 + w + gn:s2], jnp.pad(wi[:, s2:s3], ((0, 0), (0, n_pad)))], axis=1).astype(BF16)
    off = {"xs": 0, "z": w, "sch": 2 * w, "scb": 3 * w, "scc": 4 * w, "bm": 5 * w, "cm": 5 * w + gn,
           "dt": 5 * w + 2 * gn}
    cw, cb = ssd_conv_w[i], ssd_conv_b[i]
    mix = dict(dims, off=off,
               cwx=cw[:, :w], cwb=cw[:, w:w + gn], cwc=cw[:, w + gn:],
               cbx=cb[:w].reshape(1, w), cbb=cb[w:w + gn].reshape(1, gn), cbc=cb[w + gn:].reshape(1, gn),
               dtb=_pad_lanes(ssd_dt_bias[i]), alog=_pad_lanes(ssd_a_log[i]),
               dx=jnp.repeat(ssd_d[i], dims["p"]).reshape(1, w), nw=ssd_norm[i].reshape(1, w), scw=sc_conv_w[i])
    ph, _, nk, half = peer_sub_keys[i].shape
    return dict(
        norm_mix=norm_mix[i], w_in=w_in_p, mix=mix,
        w_out_ssd=w_out[i][:w].astype(BF16), w_out_sc=w_out[i][w:].astype(BF16),
        norm_mem_q=norm_mem_q[i], norm_mem_kv=norm_mem_kv[i], w_mem_q=w_mem_q[i].astype(BF16),
        w_mem_kv=jnp.concatenate([w_mem_k[i], w_mem_v[i]], axis=1).astype(BF16), w_mem_o=w_mem_o[i].astype(BF16),
        norm_ffn=norm_ffn[i], w_peer_q=w_peer_q[i].astype(BF16),
        sub_keys=peer_sub_keys[i].reshape(ph * 2, nk, half).astype(BF16),
        peer_u=peer_u[i].astype(BF16), peer_v_t=_peer_v_slabs(peer_v[i], nk), nk=nk)


def _attn_peer(x1, mem_k, mem_v, lp, seq_shape, mem_heads):
    b, l = seq_shape
    d = x1.shape[1]
    q = _rms_matmul(x1, lp["norm_mem_q"], lp["w_mem_q"])
    o = _attention(q.reshape(b, l, d), mem_k, mem_v, mem_heads).reshape(b * l, d)
    x2 = _matmul_res([o], [lp["w_mem_o"]], x1)
    pq = _rms_matmul(x2, lp["norm_ffn"], lp["w_peer_q"], out_dtype=BF16)
    xnt, rk, e2, qe = _peer_route(x2, lp["norm_ffn"], pq, lp["sub_keys"])
    delta = _peer_main(xnt, rk, e2, qe, lp["peer_u"], lp["peer_v_t"], lp["nk"])
    return x2, delta


def kernel(x_prompt, x_sample, mem_prompt, state_ssm, state_ssd_conv, state_short_conv, cache_mem_k, cache_mem_v,
           norm_mix, w_in, ssd_conv_w, ssd_conv_b, ssd_dt_bias, ssd_a_log, ssd_d, ssd_norm, sc_conv_w, w_out,
           norm_mem_q, norm_mem_kv, w_mem_q, w_mem_k, w_mem_v, w_mem_o, norm_ffn, w_peer_q, peer_sub_keys,
           peer_u, peer_v, norm_final):
    bp, lp_, d = x_prompt.shape
    bs, ls, _ = x_sample.shape
    depth, _, heads, p, n = state_ssm.shape
    k_ssd = state_ssd_conv.shape[2] + 1
    cd = state_ssd_conv.shape[3]
    k_sc = state_short_conv.shape[2] + 1
    scw = state_short_conv.shape[3]
    w = heads * p
    gn = (cd - w) // 2
    g = gn // n
    mem_tokens, mem_heads, mem_hd = cache_mem_k.shape[2:]
    assert scw == w and d == mem_heads * mem_hd
    dims = dict(w=w, gn=gn, n=n, p=p, g=g, heads=heads, scw=scw)
    lc_p = min(SSD_CHUNK, lp_)
    nb_s = 16 if (bs % 16 == 0) else SUBLANES
    assert lp_ % lc_p == 0 and ls <= SSD_CHUNK and bs % nb_s == 0

    hp = x_prompt.reshape(bp * lp_, d)
    hs = jnp.transpose(x_sample, (1, 0, 2)).reshape(ls * bs, d)
    outs = {k_: [] for k_ in ("p_ssm", "p_ssd", "p_sc", "p_mk", "p_mv", "s_ssm", "s_ssd", "s_sc")}
    for i in range(depth):
        lp = _layer_params(i, dims, norm_mix, w_in, ssd_conv_w, ssd_conv_b, ssd_dt_bias, ssd_a_log, ssd_d,
                           ssd_norm, sc_conv_w, w_out, norm_mem_q, norm_mem_kv, w_mem_q, w_mem_k, w_mem_v,
                           w_mem_o, norm_ffn, w_peer_q, peer_sub_keys, peer_u, peer_v)
        kv = _rms_matmul(mem_prompt.reshape(bp * mem_tokens, d), lp["norm_mem_kv"], lp["w_mem_kv"])
        mk = kv[:, :d].reshape(bp, mem_tokens, d)
        mv = kv[:, d:].reshape(bp, mem_tokens, d)
        h = _rms_matmul(hp, lp["norm_mix"], lp["w_in"]).reshape(bp, lp_, -1)
        y_ssd, y_sc, n_ssm, ntx, ntb, ntc, ntsc = _mixer(
            h, jnp.zeros((bp, w, n), F32), jnp.zeros((bp, k_ssd - 1, cd), F32), jnp.zeros((bp, k_sc - 1, scw), F32),
            lp["mix"], time_major=False, lc=lc_p, nb=1, gb=g)
        x1 = _matmul_res([y_ssd.reshape(bp * lp_, w), y_sc.reshape(bp * lp_, w)],
                         [lp["w_out_ssd"], lp["w_out_sc"]], hp)
        x2, delta = _attn_peer(x1, mk, mv, lp, (bp, lp_), mem_heads)
        if i == depth - 1:
            hp = _add_rms(x2, delta, norm_final)
        else:
            hp = x2 + delta
        outs["p_ssm"].append(n_ssm.reshape(bp, heads, p, n))
        outs["p_ssd"].append(jnp.concatenate([ntx, ntb, ntc], axis=-1))
        outs["p_sc"].append(ntsc)
        outs["p_mk"].append(mk.reshape(bp, mem_tokens, mem_heads, mem_hd))
        outs["p_mv"].append(mv.reshape(bp, mem_tokens, mem_heads, mem_hd))
        h = _rms_matmul(hs, lp["norm_mix"], lp["w_in"]).reshape(ls, bs, -1)
        y_ssd, y_sc, m_ssm, ntx, ntb, ntc, ntsc = _mixer(
            h, state_ssm[i].reshape(bs, w, n), jnp.transpose(state_ssd_conv[i], (1, 0, 2)),
            jnp.transpose(state_short_conv[i], (1, 0, 2)), lp["mix"], time_major=True, lc=ls, nb=nb_s, gb=1)
        x1 = _matmul_res([y_ssd.reshape(ls * bs, w), y_sc.reshape(ls * bs, w)],
                         [lp["w_out_ssd"], lp["w_out_sc"]], hs)
        x1 = jnp.transpose(x1.reshape(ls, bs, d), (1, 0, 2)).reshape(bs * ls, d)
        x2, delta = _attn_peer(x1, cache_mem_k[i], cache_mem_v[i], lp, (bs, ls), mem_heads)
        if i == depth - 1:
            hs_bm = _add_rms(x2, delta, norm_final)
        else:
            hs_bm = x2 + delta
        hs = jnp.transpose(hs_bm.reshape(bs, ls, d), (1, 0, 2)).reshape(ls * bs, d) if i < depth - 1 else hs_bm
        outs["s_ssm"].append(m_ssm.reshape(bs, heads, p, n))
        outs["s_ssd"].append(jnp.transpose(jnp.concatenate([ntx, ntb, ntc], axis=-1), (1, 0, 2)))
        outs["s_sc"].append(jnp.transpose(ntsc, (1, 0, 2)))
    y_prompt = hp.reshape(bp, lp_, d)
    y_sample = hs.reshape(bs, ls, d)
    return (y_prompt, y_sample, jnp.stack(outs["p_ssm"]), jnp.stack(outs["p_ssd"]), jnp.stack(outs["p_sc"]),
            jnp.stack(outs["p_mk"]), jnp.stack(outs["p_mv"]), jnp.stack(outs["s_ssm"]), jnp.stack(outs["s_ssd"]),
            jnp.stack(outs["s_sc"]))
```

```python
import functools
import math

import jax
import jax.numpy as jnp
import numpy as np
from jax import lax
from jax.experimental import pallas as pl
from jax.experimental.pallas import tpu as pltpu

F32 = jnp.float32
BF16 = jnp.bfloat16

EPS = 1e-6
SSD_CHUNK = 128
PEER_TOPK = 16
LANES = 128
SUBLANES = 8
VMEM_LIMIT_BYTES = 56 * 1024 * 1024
NEG_INF = float("-inf")
IN_PROJ_COL_TILE = 512
ROUTE_TOKENS = 512
PEER_GATE_LANES = 256
PEER_EXPERT_TILE = 1024


def _cparams(sem, flags=None):
    return pltpu.CompilerParams(dimension_semantics=sem, vmem_limit_bytes=VMEM_LIMIT_BYTES, flags=flags)


def _tile(n, pref, mult):
    if n <= pref:
        return n
    t = (pref // mult) * mult
    while t >= mult:
        if n % t == 0:
            return t
        t -= mult
    return n


def _hdot(a, b):
    return jnp.dot(a, b, precision=lax.Precision.HIGHEST, preferred_element_type=F32)


def _hdot_nt(a, b):
    return lax.dot_general(a, b, (((1,), (1,)), ((), ())), precision=lax.Precision.HIGHEST,
                           preferred_element_type=F32)


def _dot(a, b):
    return jnp.dot(a, b, preferred_element_type=F32)


def _dot_nt(a, b):
    return lax.dot_general(a, b, (((1,), (1,)), ((), ())), preferred_element_type=F32)


def _silu(x):
    return x * (1.0 / (1.0 + jnp.exp(-x)))


def _softplus(x):
    return jnp.maximum(x, 0.0) + jnp.log1p(jnp.exp(-jnp.abs(x)))


def _gelu_exact(x):
    return 0.5 * x * (1.0 + lax.erf(x * (1.0 / math.sqrt(2.0))))


def _rms_matmul_kernel(x_ref, g_ref, w_ref, o_ref, xn_ref):
    @pl.when(pl.program_id(1) == 0)
    def _():
        x = x_ref[...]
        ms = jnp.mean(x * x, axis=-1, keepdims=True)
        xn_ref[...] = (x * lax.rsqrt(ms + EPS) * g_ref[...]).astype(BF16)

    o_ref[...] = _dot(xn_ref[...], w_ref[...]).astype(o_ref.dtype)


def _rms_matmul(x, g, w, out_dtype=F32):
    t, d = x.shape
    n = w.shape[1]
    tm = _tile(t, 1024, SUBLANES)
    tn = 1024 if n % 1024 == 0 else _tile(n, 512, LANES)
    return pl.pallas_call(
        _rms_matmul_kernel,
        grid=(t // tm, n // tn),
        in_specs=[pl.BlockSpec((tm, d), lambda i, j: (i, 0)),
                  pl.BlockSpec((1, d), lambda i, j: (0, 0)),
                  pl.BlockSpec((d, tn), lambda i, j: (0, j))],
        out_specs=pl.BlockSpec((tm, tn), lambda i, j: (i, j)),
        out_shape=jax.ShapeDtypeStruct((t, n), out_dtype),
        scratch_shapes=[pltpu.VMEM((tm, d), BF16)],
        compiler_params=_cparams(("parallel", "arbitrary")),
        name="rms_matmul",
    )(x, g.reshape(1, d), w)


def _matmul_res_kernel(*refs, n_a):
    a_refs = refs[:n_a]
    w_refs = refs[n_a:2 * n_a]
    r_ref, o_ref = refs[2 * n_a], refs[2 * n_a + 1]
    ab_refs = refs[2 * n_a + 2:]

    @pl.when(pl.program_id(1) == 0)
    def _():
        for a_ref, ab_ref in zip(a_refs, ab_refs):
            ab_ref[...] = a_ref[...].astype(BF16)

    acc = r_ref[...]
    for ab_ref, w_ref in zip(ab_refs, w_refs):
        acc = acc + _dot(ab_ref[...], w_ref[...])
    o_ref[...] = acc


def _matmul_res(a_list, w_list, res):
    t, d = res.shape
    tm = _tile(t, 1024, SUBLANES)
    tn = _tile(d, 512, LANES)
    n_a = len(a_list)
    in_specs = ([pl.BlockSpec((tm, a.shape[1]), lambda i, j: (i, 0)) for a in a_list]
                + [pl.BlockSpec((w.shape[0], tn), lambda i, j: (0, j)) for w in w_list]
                + [pl.BlockSpec((tm, tn), lambda i, j: (i, j))])
    return pl.pallas_call(
        functools.partial(_matmul_res_kernel, n_a=n_a),
        grid=(t // tm, d // tn),
        in_specs=in_specs,
        out_specs=pl.BlockSpec((tm, tn), lambda i, j: (i, j)),
        out_shape=jax.ShapeDtypeStruct((t, d), F32),
        scratch_shapes=[pltpu.VMEM((tm, a.shape[1]), BF16) for a in a_list],
        compiler_params=_cparams(("parallel", "arbitrary")),
        name="matmul_res",
    )(*a_list, *w_list, res)


def _mixer_kernel(xs_ref, z_ref, sch_ref, scb_ref, scc_ref, bm_ref, cm_ref, dt_ref,
                  s0_ref, tx_ref, tb_ref, tc_ref, tsc_ref,
                  cwx_ref, cwb_ref, cwc_ref, cbx_ref, cbb_ref, cbc_ref,
                  dtb_ref, alog_ref, sel_ref, dx_ref, nw_ref, scw_ref,
                  e_ref, et_ref, tri_ref, blk_ref, eye_ref,
                  yssd_ref, ysc_ref, sout_ref, ntx_ref, ntb_ref, ntc_ref, ntsc_ref,
                  s_scr, padx, padb, padc, padu,
                  *, lc, nb, gb, rh, p, n, n_chunks, time_major):
    rows = lc * nb
    gw = rh * p
    hr = padx.shape[0] - rows
    k_ssd = cwx_ref.shape[0]
    k_sc = scw_ref.shape[0]
    c = pl.program_id(2)

    def ld(ref):
        if time_major:
            return ref[...].reshape(ref.shape[0] * ref.shape[1], ref.shape[2])
        return ref[0]

    def st(ref, val):
        if time_major:
            ref[...] = val.reshape(ref.shape)
        else:
            ref[0] = val

    @pl.when(c == 0)
    def _():
        s_scr[...] = s0_ref[...]
        padx[hr - (k_ssd - 1) * nb:hr, :] = ld(tx_ref)
        padb[hr - (k_ssd - 1) * nb:hr, :] = ld(tb_ref)
        padc[hr - (k_ssd - 1) * nb:hr, :] = ld(tc_ref)
        padu[hr - (k_sc - 1) * nb:hr, :] = ld(tsc_ref)

    def conv(pad, x, w_ref, nk):
        pad[hr:hr + rows, :] = x
        acc = None
        for j in range(nk):
            off = hr - (nk - 1 - j) * nb
            term = pad[off:off + rows, :] * w_ref[j:j + 1, :]
            acc = term if acc is None else acc + term
        tail = pad[hr + rows - (nk - 1) * nb:hr + rows, :]
        if n_chunks > 1:
            pad[hr - (nk - 1) * nb:hr, :] = tail
        return acc, tail

    xs_pre, tail_x = conv(padx, ld(xs_ref), cwx_ref, k_ssd)
    b_pre, tail_b = conv(padb, ld(bm_ref), cwb_ref, k_ssd)
    c_pre, tail_c = conv(padc, ld(cm_ref), cwc_ref, k_ssd)
    cv, tail_u = conv(padu, ld(scc_ref) * ld(sch_ref), scw_ref, k_sc)
    st(ntx_ref, tail_x)
    st(ntb_ref, tail_b)
    st(ntc_ref, tail_c)
    st(ntsc_ref, tail_u)
    st(ysc_ref, ld(scb_ref) * cv)

    xs_c = _silu(xs_pre + cbx_ref[...])
    b_c = _silu(b_pre + cbb_ref[...])
    c_c = _silu(c_pre + cbc_ref[...])

    dt_all = _softplus(ld(dt_ref) + dtb_ref[...])
    a_all = dt_all * (-jnp.exp(alog_ref[...]))
    sel = sel_ref[0]
    dt = _hdot(dt_all, sel)
    a = _hdot(a_all, sel)
    tri = tri_ref[...]
    a_cs = _hdot(tri, a)
    a_tot = _hdot(blk_ref[...], a)
    e = e_ref[...]
    dt_x = _hdot(dt, e)
    wst_x = _hdot(jnp.exp(a_tot - a_cs), e)
    woff_x = _hdot(jnp.exp(a_cs), e)
    xdt = xs_c * dt_x
    xw = xdt * wst_x
    eye = eye_ref[...]
    a_cs_t = _hdot_nt(eye, a_cs)
    tri_mask = tri > 0.5
    lane_id = lax.broadcasted_iota(jnp.int32, (rows, LANES), 1)
    row_seq_g = lax.broadcasted_iota(jnp.int32, (rows, gw), 0) % nb
    row_seq_n = lax.broadcasted_iota(jnp.int32, (rows, n), 0) % nb

    dec_cols = _hdot(et_ref[...], jnp.exp(_hdot_nt(eye, a_tot)))
    decs = [jnp.broadcast_to(dec_cols[:, b:b + 1], (gb * gw, n)) for b in range(nb)]

    z = ld(z_ref)
    y_cols = []
    for gl in range(gb):
        cg = c_c[:, gl * n:(gl + 1) * n].astype(BF16)
        bg = b_c[:, gl * n:(gl + 1) * n]
        cb = _dot_nt(cg, bg.astype(BF16))
        ydiag = []
        for pr in range(gw // LANES):
            xpair = xdt[:, gl * gw + pr * LANES:gl * gw + (pr + 1) * LANES]
            acc = None
            for sub in range(LANES // p):
                hl = gl * rh + pr * (LANES // p) + sub
                col = a_cs[:, hl:hl + 1]
                row = a_cs_t[hl:hl + 1, :]
                lmat = jnp.exp(jnp.where(tri_mask, col - row, NEG_INF))
                m = (cb * lmat).astype(BF16)
                xm = jnp.where((lane_id >= sub * p) & (lane_id < (sub + 1) * p), xpair, 0.0).astype(BF16)
                t = _dot(m, xm)
                acc = t if acc is None else acc + t
            ydiag.append(acc)
        ydiag = ydiag[0] if len(ydiag) == 1 else jnp.concatenate(ydiag, axis=1)

        xw_t = xw[:, gl * gw:(gl + 1) * gw].T.astype(BF16)
        yoff = None
        for b in range(nb):
            s_b = s_scr[b, gl * gw:(gl + 1) * gw, :]
            y_b = _dot_nt(cg, s_b.astype(BF16))
            if nb > 1:
                y_b = jnp.where(row_seq_g == b, y_b, 0.0)
                bg_b = jnp.where(row_seq_n == b, bg, 0.0).astype(BF16)
            else:
                bg_b = bg.astype(BF16)
            yoff = y_b if yoff is None else yoff + y_b
            s_scr[b, gl * gw:(gl + 1) * gw, :] = decs[b][gl * gw:(gl + 1) * gw, :] * s_b + _dot(xw_t, bg_b)

        sl = slice(gl * gw, (gl + 1) * gw)
        y = ydiag + yoff * woff_x[:, sl] + dx_ref[:, sl] * xs_c[:, sl]
        yg = y * _silu(z[:, sl])
        ms = jnp.mean(yg * yg, axis=-1, keepdims=True)
        y_cols.append(yg * lax.rsqrt(ms + EPS) * nw_ref[:, sl])
    y_all = y_cols[0] if gb == 1 else jnp.concatenate(y_cols, axis=1)
    st(yssd_ref, y_all)

    @pl.when(c == n_chunks - 1)
    def _():
        sout_ref[...] = s_scr[...]


def _mixer(h, s0, tail_ssd, tail_sc, prm, *, time_major, lc, nb, gb):
    w, gn, n, p, g = prm["w"], prm["gn"], prm["n"], prm["p"], prm["g"]
    rh = (w // p) // g
    gw = rh * p
    bw = gb * gw
    bn = gb * n
    if time_major:
        seq_len, n_seq = h.shape[0], h.shape[1]
        n_chunks = 1
        assert seq_len == lc
    else:
        n_seq, seq_len = h.shape[0], h.shape[1]
        n_chunks = seq_len // lc
        assert nb == 1 and seq_len % lc == 0
    rows = lc * nb
    k_ssd = prm["cwx"].shape[0]
    k_sc = prm["scw"].shape[0]
    assert lc >= k_ssd - 1 and n_seq % nb == 0 and g % gb == 0
    assert w % bw == 0 and w % bn == 0 and (w + gn) % bn == 0 and LANES % p == 0 and gw % LANES == 0
    hr = -(-((k_ssd - 1) * nb) // SUBLANES) * SUBLANES
    off = prm["off"]

    def hspec(width, col0):
        assert col0 % width == 0
        cb0 = col0 // width
        if time_major:
            return pl.BlockSpec((lc, nb, width), lambda i, j, c: (0, i, cb0 + j))
        return pl.BlockSpec((1, lc, width), lambda i, j, c: (i, c, cb0 + j))

    def hspec_fixed(width, col0):
        cb0 = col0 // width
        if time_major:
            return pl.BlockSpec((lc, nb, width), lambda i, j, c: (0, i, cb0))
        return pl.BlockSpec((1, lc, width), lambda i, j, c: (i, c, cb0))

    def tspec(k, width, col0):
        assert col0 % width == 0
        cb0 = col0 // width
        if time_major:
            return pl.BlockSpec((k, nb, width), lambda i, j, c: (0, i, cb0 + j))
        return pl.BlockSpec((1, k, width), lambda i, j, c: (i, 0, cb0 + j))

    def ospec(width):
        if time_major:
            return pl.BlockSpec((lc, nb, width), lambda i, j, c: (0, i, j))
        return pl.BlockSpec((1, lc, width), lambda i, j, c: (i, c, j))

    def otspec(k, width):
        if time_major:
            return pl.BlockSpec((k, nb, width), lambda i, j, c: (0, i, j))
        return pl.BlockSpec((1, k, width), lambda i, j, c: (i, 0, j))

    def pspec(rows_, width):
        return pl.BlockSpec((rows_, width), lambda i, j, c: (0, j))

    def cspec(shape):
        return pl.BlockSpec(shape, lambda i, j, c: (0,) * len(shape))

    def oshape(k, width):
        return (k, n_seq, width) if time_major else (n_seq, k, width)

    r = np.arange(rows)
    same = (r[:, None] % nb) == (r[None, :] % nb)
    tri = (same & ((r[None, :] // nb) <= (r[:, None] // nb))).astype(np.float32)
    blk = same.astype(np.float32)
    hl = gb * rh
    e_mat = np.zeros((LANES, bw), np.float32)
    e_mat[np.arange(bw) // p, np.arange(bw)] = 1.0
    sel = np.zeros((g // gb, LANES, LANES), np.float32)
    for j in range(g // gb):
        sel[j, j * hl + np.arange(hl), np.arange(hl)] = 1.0

    in_specs = [
        hspec(bw, off["xs"]), hspec(bw, off["z"]), hspec(bw, off["sch"]), hspec(bw, off["scb"]),
        hspec(bw, off["scc"]), hspec(bn, off["bm"]), hspec(bn, off["cm"]), hspec_fixed(LANES, off["dt"]),
        pl.BlockSpec((nb, bw, n), lambda i, j, c: (i, j, 0)),
        tspec(k_ssd - 1, bw, 0), tspec(k_ssd - 1, bn, w), tspec(k_ssd - 1, bn, w + gn), tspec(k_sc - 1, bw, 0),
        pspec(k_ssd, bw), pspec(k_ssd, bn), pspec(k_ssd, bn), pspec(1, bw), pspec(1, bn), pspec(1, bn),
        cspec((1, LANES)), cspec((1, LANES)),
        pl.BlockSpec((1, LANES, LANES), lambda i, j, c: (j, 0, 0)),
        pspec(1, bw), pspec(1, bw), pspec(k_sc, bw),
        cspec((LANES, bw)), cspec((bw, LANES)), cspec((rows, rows)), cspec((rows, rows)), cspec((LANES, LANES)),
    ]
    out_specs = [
        ospec(bw), ospec(bw),
        pl.BlockSpec((nb, bw, n), lambda i, j, c: (i, j, 0)),
        otspec(k_ssd - 1, bw), otspec(k_ssd - 1, bn), otspec(k_ssd - 1, bn), otspec(k_sc - 1, bw),
    ]
    out_shape = [
        jax.ShapeDtypeStruct(h.shape[:2] + (w,), F32), jax.ShapeDtypeStruct(h.shape[:2] + (w,), F32),
        jax.ShapeDtypeStruct((n_seq, w, n), F32),
        jax.ShapeDtypeStruct(oshape(k_ssd - 1, w), F32), jax.ShapeDtypeStruct(oshape(k_ssd - 1, gn), F32),
        jax.ShapeDtypeStruct(oshape(k_ssd - 1, gn), F32), jax.ShapeDtypeStruct(oshape(k_sc - 1, w), F32),
    ]
    scratch = [pltpu.VMEM((nb, bw, n), F32), pltpu.VMEM((hr + rows, bw), F32), pltpu.VMEM((hr + rows, bn), F32),
               pltpu.VMEM((hr + rows, bn), F32), pltpu.VMEM((hr + rows, bw), F32)]
    kern = functools.partial(_mixer_kernel, lc=lc, nb=nb, gb=gb, rh=rh, p=p, n=n, n_chunks=n_chunks,
                             time_major=time_major)
    return pl.pallas_call(
        kern,
        grid=(n_seq // nb, g // gb, n_chunks),
        in_specs=in_specs, out_specs=out_specs, out_shape=out_shape, scratch_shapes=scratch,
        compiler_params=_cparams(("parallel", "parallel", "arbitrary")),
        name="mixer_tm" if time_major else "mixer",
    )(h, h, h, h, h, h, h, h, s0, tail_ssd, tail_ssd, tail_ssd, tail_sc,
      prm["cwx"], prm["cwb"], prm["cwc"], prm["cbx"], prm["cbb"], prm["cbc"],
      prm["dtb"], prm["alog"], jnp.asarray(sel), prm["dx"], prm["nw"], prm["scw"],
      jnp.asarray(e_mat), jnp.asarray(e_mat.T), jnp.asarray(tri), jnp.asarray(blk),
      jnp.asarray(np.eye(LANES, dtype=np.float32)))


def _attn_kernel(q_ref, k_ref, v_ref, o_ref, kb_ref, vb_ref, *, heads):
    d = q_ref.shape[-1]
    hd = d // heads

    @pl.when(pl.program_id(1) == 0)
    def _():
        if len(k_ref.shape) == 4:
            for h in range(heads):
                kb_ref[:, h * hd:(h + 1) * hd] = k_ref[0, :, h, :].astype(BF16)
                vb_ref[:, h * hd:(h + 1) * hd] = v_ref[0, :, h, :].astype(BF16)
        else:
            kb_ref[...] = k_ref[0].astype(BF16)
            vb_ref[...] = v_ref[0].astype(BF16)

    scale = 1.0 / math.sqrt(hd)
    for h in range(heads):
        sl = slice(h * hd, (h + 1) * hd)
        s = _dot_nt(q_ref[0, :, sl].astype(BF16), kb_ref[:, sl]) * scale
        m = jnp.max(s, axis=-1, keepdims=True)
        pexp = jnp.exp(s - m)
        prob = pexp / jnp.sum(pexp, axis=-1, keepdims=True)
        o_ref[0, :, sl] = _dot(prob.astype(BF16), vb_ref[:, sl])


def _attention(q, k, v, heads):
    b, l, d = q.shape
    m = k.shape[1]
    tq = _tile(l, 512, SUBLANES)
    kv_block = (1,) + k.shape[1:]
    kv_spec = pl.BlockSpec(kv_block, lambda i, j: (i,) + (0,) * (len(kv_block) - 1))
    return pl.pallas_call(
        functools.partial(_attn_kernel, heads=heads),
        grid=(b, l // tq),
        in_specs=[pl.BlockSpec((1, tq, d), lambda i, j: (i, j, 0)), kv_spec, kv_spec],
        out_specs=pl.BlockSpec((1, tq, d), lambda i, j: (i, j, 0)),
        out_shape=jax.ShapeDtypeStruct((b, l, d), F32),
        scratch_shapes=[pltpu.VMEM((m, d), BF16), pltpu.VMEM((m, d), BF16)],
        compiler_params=_cparams(("parallel", "arbitrary")),
        name="mem_attn",
    )(q, k, v)


def _top_extract(s, iota_f, k):
    rank = jnp.full(s.shape, 255.0, F32)
    vals = []
    for r in range(k):
        m = jnp.max(s, axis=0, keepdims=True)
        idx = jnp.min(jnp.where(s == m, iota_f, 1e9), axis=0, keepdims=True)
        hit = iota_f == idx
        rank = jnp.where(hit, float(r), rank)
        s = jnp.where(hit, NEG_INF, s)
        vals.append(m)
    return rank, vals


def _bf16_words(x):
    bits = pltpu.bitcast(x.astype(BF16).astype(F32), jnp.uint32)
    return pltpu.bitcast(bits | (bits >> 16), jnp.int32)


def _peer_route_kernel(x_ref, g_ref, q_ref, sk_ref, xnt_ref, rk_ref, e2_ref, qe_ref, *, heads, k):
    x = x_ref[...]
    ms = jnp.mean(x * x, axis=-1, keepdims=True)
    xn = x * lax.rsqrt(ms + EPS) * g_ref[...]
    xnt_ref[...] = xn.T.astype(BF16)
    nk, half = sk_ref.shape[1], sk_ref.shape[2]
    tm = x.shape[0]
    iota_k = lax.broadcasted_iota(jnp.int32, (nk, tm), 0).astype(F32)

    kh = k // 2
    n_cand = k + (kh - 1) * kh + kh
    row_id = lax.broadcasted_iota(jnp.int32, (n_cand, tm), 0)
    pos = jnp.zeros((n_cand, tm), F32)
    pos = jnp.where(row_id < k, row_id.astype(F32), pos)
    for p_ in range(1, kh):
        lo = k + (p_ - 1) * kh
        pos = jnp.where((row_id >= lo) & (row_id < lo + kh), (row_id - lo + p_ * k).astype(F32), pos)
    lo = k + (kh - 1) * kh
    pos = jnp.where(row_id >= lo, ((row_id - lo + kh) * k).astype(F32), pos)

    def head_body(h, carry):
        s_half = []
        for half_i in range(2):
            col0 = pl.multiple_of((h * 2 + half_i) * half, half)
            s_half.append(_dot_nt(sk_ref[h * 2 + half_i], q_ref[:, pl.ds(col0, half)]))
        rank1, a = _top_extract(s_half[0], iota_k, k)
        rank2, b = _top_extract(s_half[1], iota_k, k)
        b_lo = jnp.concatenate(b[:kh], axis=0)
        cand = jnp.concatenate([a[0] + jnp.concatenate(b, axis=0)]
                               + [a[p_] + b_lo for p_ in range(1, kh)]
                               + [jnp.concatenate(a[kh:], axis=0) + b[0]], axis=0)
        c_max = a[0] + b[0]
        sel = jnp.zeros(cand.shape, F32)
        cur = cand
        for _ in range(k):
            m = jnp.max(cur, axis=0, keepdims=True)
            idx = jnp.min(jnp.where(cur == m, pos, 1e9), axis=0, keepdims=True)
            hit = pos == idx
            sel = jnp.where(hit, 1.0, sel)
            cur = jnp.where(hit, NEG_INF, cur)
        zsum = jnp.sum(sel * jnp.exp(cand - c_max), axis=0, keepdims=True)
        q_rows = [jnp.sum(sel[0:k], axis=0, keepdims=True)]
        for p_ in range(1, kh):
            lo_ = k + (p_ - 1) * kh
            q_rows.append(jnp.sum(sel[lo_:lo_ + kh], axis=0, keepdims=True))
        lo_ = k + (kh - 1) * kh
        for p_ in range(kh, k):
            q_rows.append(sel[lo_ + p_ - kh:lo_ + p_ - kh + 1])
        qmap = jnp.zeros((nk, tm), F32)
        for p_ in range(k):
            qmap = jnp.where(rank1 == float(p_), q_rows[p_], qmap)
        rk_ref[h] = rank2.astype(BF16)
        e2_ref[h] = jnp.exp(s_half[1] - b[0]).astype(BF16)
        qe_ref[0, h] = _bf16_words(qmap)
        qe_ref[1, h] = _bf16_words(jnp.exp(s_half[0] - a[0]) * (1.0 / zsum))
        return carry

    lax.fori_loop(0, heads, head_body, 0)


def _peer_route(x, g, q, sk):
    t, d = x.shape
    tm = ROUTE_TOKENS if t % ROUTE_TOKENS == 0 else LANES
    heads2, nk, half = sk.shape
    heads = heads2 // 2
    return pl.pallas_call(
        functools.partial(_peer_route_kernel, heads=heads, k=PEER_TOPK),
        grid=(t // tm,),
        in_specs=[pl.BlockSpec((tm, d), lambda i: (i, 0)),
                  pl.BlockSpec((1, d), lambda i: (0, 0)),
                  pl.BlockSpec((tm, q.shape[1]), lambda i: (i, 0)),
                  pl.BlockSpec(sk.shape, lambda i: (0, 0, 0))],
        out_specs=[pl.BlockSpec((d, tm), lambda i: (0, i)),
                   pl.BlockSpec((heads, nk, tm), lambda i: (0, 0, i)),
                   pl.BlockSpec((heads, nk, tm), lambda i: (0, 0, i)),
                   pl.BlockSpec((2, heads, nk, tm), lambda i: (0, 0, 0, i))],
        out_shape=[jax.ShapeDtypeStruct((d, t), BF16), jax.ShapeDtypeStruct((heads, nk, t), BF16),
                   jax.ShapeDtypeStruct((heads, nk, t), BF16), jax.ShapeDtypeStruct((2, heads, nk, t), jnp.int32)],
        compiler_params=_cparams(("parallel",)),
        name="peer_route",
    )(x, g.reshape(1, d), q, sk)


def _peer_main_kernel(xnt_ref, rk_ref, e2_ref, qe_ref, u_ref, vt_ref, o_ref, qrow, *bufs, heads, nk):
    n_ch = len(bufs) // 3
    accs, hts, ps = bufs[:n_ch], bufs[n_ch:2 * n_ch], bufs[2 * n_ch:]
    s = pl.program_id(1)
    te, tc = hts[0].shape
    ib = te // nk

    @pl.when(s == 0)
    def _():
        for acc in accs:
            acc[...] = jnp.zeros_like(acc)

    for ii in range(ib):
        for h in range(heads):
            for kind in range(2):
                r = (ii * heads + h) * 2 + kind
                qrow[r:r + 1, :] = qe_ref[kind, h, pl.ds(s * ib + ii, 1), :]

    def scores(c):
        hts[c][...] = _dot(u_ref[...], xnt_ref[:, c * tc:(c + 1) * tc])

    def accumulate(c):
        accs[c][...] += _dot(vt_ref[0], ps[c][...])

    scores(0)
    for c, (ht, p) in enumerate(zip(hts, ps)):
        cols = slice(c * tc, (c + 1) * tc)
        for ii in range(ib):
            rows = slice(ii * nk, (ii + 1) * nk)
            act = _gelu_exact(ht[rows, :]).astype(BF16)
            gate = None
            for h in range(heads):
                r = (ii * heads + h) * 2
                q_w = jnp.broadcast_to(qrow[r:r + 1, cols], (SUBLANES, tc))
                e_w = jnp.broadcast_to(qrow[r + 1:r + 2, cols], (SUBLANES, tc))
                q_b = jnp.tile(pltpu.bitcast(q_w, BF16), (nk // (2 * SUBLANES), 1))
                e_b = jnp.tile(pltpu.bitcast(e_w, BF16), (nk // (2 * SUBLANES), 1))
                t = jnp.where(rk_ref[h, :, cols] < q_b, e2_ref[h, :, cols], 0.0) * e_b
                gate = t if gate is None else gate + t
            p[rows, :] = gate * act
        if c + 1 < n_ch:
            scores(c + 1)
        if c >= 1:
            accumulate(c - 1)
    accumulate(n_ch - 1)

    @pl.when(s == pl.num_programs(1) - 1)
    def _():
        for c, acc in enumerate(accs):
            o_ref[c * tc:(c + 1) * tc, :] = acc[...].T


def _peer_expert_tile(n_exp, nk):
    return _tile(n_exp, PEER_EXPERT_TILE, nk)


def _peer_v_slabs(v, nk):
    n_exp, d = v.shape
    te = _peer_expert_tile(n_exp, nk)
    return jnp.transpose(v.reshape(n_exp // te, te, d), (0, 2, 1)).astype(BF16)


def _peer_main(xnt, rk, e2, qe, u, vt, nk):
    d, t = xnt.shape
    n_exp = u.shape[0]
    heads = rk.shape[0]
    tm = _tile(t, 512, LANES)
    te = _peer_expert_tile(n_exp, nk)
    tc = min(tm, PEER_GATE_LANES)
    n_ch = tm // tc
    assert nk % (2 * SUBLANES) == 0 and vt.shape == (n_exp // te, d, te) and tm % tc == 0
    return pl.pallas_call(
        functools.partial(_peer_main_kernel, heads=heads, nk=nk),
        grid=(t // tm, n_exp // te),
        in_specs=[pl.BlockSpec((d, tm), lambda i, s: (0, i)),
                  pl.BlockSpec((heads, nk, tm), lambda i, s: (0, 0, i)),
                  pl.BlockSpec((heads, nk, tm), lambda i, s: (0, 0, i)),
                  pl.BlockSpec((2, heads, nk, tm), lambda i, s: (0, 0, 0, i)),
                  pl.BlockSpec((te, d), lambda i, s: (s, 0)),
                  pl.BlockSpec((1, d, te), lambda i, s: (s, 0, 0))],
        out_specs=pl.BlockSpec((tm, d), lambda i, s: (i, 0)),
        out_shape=jax.ShapeDtypeStruct((t, d), F32),
        scratch_shapes=([pltpu.VMEM((2 * heads * (te // nk), tm), jnp.int32)]
                        + [pltpu.VMEM((d, tc), F32)] * n_ch + [pltpu.VMEM((te, tc), F32)] * n_ch
                        + [pltpu.VMEM((te, tc), BF16)] * n_ch),
        compiler_params=_cparams(("parallel", "arbitrary")),
        name="peer_main",
    )(xnt, rk, e2, qe, u, vt)


def _add_rms_kernel(x_ref, d_ref, g_ref, o_ref):
    x = x_ref[...] + d_ref[...]
    ms = jnp.mean(x * x, axis=-1, keepdims=True)
    o_ref[...] = x * lax.rsqrt(ms + EPS) * g_ref[...]


def _add_rms(x, delta, g):
    t, d = x.shape
    tm = _tile(t, 512, SUBLANES)
    return pl.pallas_call(
        _add_rms_kernel,
        grid=(t // tm,),
        in_specs=[pl.BlockSpec((tm, d), lambda i: (i, 0)), pl.BlockSpec((tm, d), lambda i: (i, 0)),
                  pl.BlockSpec((1, d), lambda i: (0, 0))],
        out_specs=pl.BlockSpec((tm, d), lambda i: (i, 0)),
        out_shape=jax.ShapeDtypeStruct((t, d), F32),
        compiler_params=_cparams(("parallel",)),
        name="add_rms",
    )(x, delta, g.reshape(1, d))


def _pad_lanes(v):
    return jnp.pad(v, (0, LANES - v.shape[0])).reshape(1, LANES)


def _layer_params(i, dims, norm_mix, w_in, ssd_conv_w, ssd_conv_b, ssd_dt_bias, ssd_a_log, ssd_d, ssd_norm,
                  sc_conv_w, w_out, norm_mem_q, norm_mem_kv, w_mem_q, w_mem_k, w_mem_v, w_mem_o, norm_ffn,
                  w_peer_q, peer_sub_keys, peer_u, peer_v):
    w, gn, hh, scw = dims["w"], dims["gn"], dims["heads"], dims["scw"]
    cd = w + 2 * gn
    s1, s2, s3 = w, w + cd, w + cd + hh
    s4, s5 = s3 + scw, s3 + 2 * scw
    wi = w_in[i]
    n_used = 5 * w + 2 * gn + hh
    n_pad = -(-n_used // IN_PROJ_COL_TILE) * IN_PROJ_COL_TILE - n_used
    w_in_p = jnp.concatenate(
        [wi[:, s1:s1 + w], wi[:, :s1], wi[:, s3:s4], wi[:, s4:s5], wi[:, s5:], wi[:, s1 + w:s1 + w + gn],
         wi[:, s1 + w + gn:s2], jnp.pad(wi[:, s2:s3], ((0, 0), (0, n_pad)))], axis=1).astype(BF16)
    off = {"xs": 0, "z": w, "sch": 2 * w, "scb": 3 * w, "scc": 4 * w, "bm": 5 * w, "cm": 5 * w + gn,
           "dt": 5 * w + 2 * gn}
    cw, cb = ssd_conv_w[i], ssd_conv_b[i]
    mix = dict(dims, off=off,
               cwx=cw[:, :w], cwb=cw[:, w:w + gn], cwc=cw[:, w + gn:],
               cbx=cb[:w].reshape(1, w), cbb=cb[w:w + gn].reshape(1, gn), cbc=cb[w + gn:].reshape(1, gn),
               dtb=_pad_lanes(ssd_dt_bias[i]), alog=_pad_lanes(ssd_a_log[i]),
               dx=jnp.repeat(ssd_d[i], dims["p"]).reshape(1, w), nw=ssd_norm[i].reshape(1, w), scw=sc_conv_w[i])
    ph, _, nk, half = peer_sub_keys[i].shape
    return dict(
        norm_mix=norm_mix[i], w_in=w_in_p, mix=mix,
        w_out_ssd=w_out[i][:w].astype(BF16), w_out_sc=w_out[i][w:].astype(BF16),
        norm_mem_q=norm_mem_q[i], norm_mem_kv=norm_mem_kv[i], w_mem_q=w_mem_q[i].astype(BF16),
        w_mem_kv=jnp.concatenate([w_mem_k[i], w_mem_v[i]], axis=1).astype(BF16), w_mem_o=w_mem_o[i].astype(BF16),
        norm_ffn=norm_ffn[i], w_peer_q=w_peer_q[i].astype(BF16),
        sub_keys=peer_sub_keys[i].reshape(ph * 2, nk, half).astype(BF16),
        peer_u=peer_u[i].astype(BF16), peer_v_t=_peer_v_slabs(peer_v[i], nk), nk=nk)


def _attn_peer(x1, mem_k, mem_v, lp, seq_shape, mem_heads):
    b, l = seq_shape
    d = x1.shape[1]
    q = _rms_matmul(x1, lp["norm_mem_q"], lp["w_mem_q"])
    o = _attention(q.reshape(b, l, d), mem_k, mem_v, mem_heads).reshape(b * l, d)
    x2 = _matmul_res([o], [lp["w_mem_o"]], x1)
    pq = _rms_matmul(x2, lp["norm_ffn"], lp["w_peer_q"], out_dtype=BF16)
    xnt, rk, e2, qe = _peer_route(x2, lp["norm_ffn"], pq, lp["sub_keys"])
    delta = _peer_main(xnt, rk, e2, qe, lp["peer_u"], lp["peer_v_t"], lp["nk"])
    return x2, delta


def kernel(x_prompt, x_sample, mem_prompt, state_ssm, state_ssd_conv, state_short_conv, cache_mem_k, cache_mem_v,
           norm_mix, w_in, ssd_conv_w, ssd_conv_b, ssd_dt_bias, ssd_a_log, ssd_d, ssd_norm, sc_conv_w, w_out,
           norm_mem_q, norm_mem_kv, w_mem_q, w_mem_k, w_mem_v, w_mem_o, norm_ffn, w_peer_q, peer_sub_keys,
           peer_u, peer_v, norm_final):
    bp, lp_, d = x_prompt.shape
    bs, ls, _ = x_sample.shape
    depth, _, heads, p, n = state_ssm.shape
    k_ssd = state_ssd_conv.shape[2] + 1
    cd = state_ssd_conv.shape[3]
    k_sc = state_short_conv.shape[2] + 1
    scw = state_short_conv.shape[3]
    w = heads * p
    gn = (cd - w) // 2
    g = gn // n
    mem_tokens, mem_heads, mem_hd = cache_mem_k.shape[2:]
    assert scw == w and d == mem_heads * mem_hd
    dims = dict(w=w, gn=gn, n=n, p=p, g=g, heads=heads, scw=scw)
    lc_p = min(SSD_CHUNK, lp_)
    nb_s = 16 if (bs % 16 == 0) else SUBLANES
    assert lp_ % lc_p == 0 and ls <= SSD_CHUNK and bs % nb_s == 0

    hp = x_prompt.reshape(bp * lp_, d)
    hs = jnp.transpose(x_sample, (1, 0, 2)).reshape(ls * bs, d)
    outs = {k_: [] for k_ in ("p_ssm", "p_ssd", "p_sc", "p_mk", "p_mv", "s_ssm", "s_ssd", "s_sc")}
    for i in range(depth):
        lp = _layer_params(i, dims, norm_mix, w_in, ssd_conv_w, ssd_conv_b, ssd_dt_bias, ssd_a_log, ssd_d,
                           ssd_norm, sc_conv_w, w_out, norm_mem_q, norm_mem_kv, w_mem_q, w_mem_k, w_mem_v,
                           w_mem_o, norm_ffn, w_peer_q, peer_sub_keys, peer_u, peer_v)
        kv = _rms_matmul(mem_prompt.reshape(bp * mem_tokens, d), lp["norm_mem_kv"], lp["w_mem_kv"])
        mk = kv[:, :d].reshape(bp, mem_tokens, d)
        mv = kv[:, d:].reshape(bp, mem_tokens, d)
        h = _rms_matmul(hp, lp["norm_mix"], lp["w_in"]).reshape(bp, lp_, -1)
        y_ssd, y_sc, n_ssm, ntx, ntb, ntc, ntsc = _mixer(
            h, jnp.zeros((bp, w, n), F32), jnp.zeros((bp, k_ssd - 1, cd), F32), jnp.zeros((bp, k_sc - 1, scw), F32),
            lp["mix"], time_major=False, lc=lc_p, nb=1, gb=g)
        x1 = _matmul_res([y_ssd.reshape(bp * lp_, w), y_sc.reshape(bp * lp_, w)],
                         [lp["w_out_ssd"], lp["w_out_sc"]], hp)
        x2, delta = _attn_peer(x1, mk, mv, lp, (bp, lp_), mem_heads)
        if i == depth - 1:
            hp = _add_rms(x2, delta, norm_final)
        else:
            hp = x2 + delta
        outs["p_ssm"].append(n_ssm.reshape(bp, heads, p, n))
        outs["p_ssd"].append(jnp.concatenate([ntx, ntb, ntc], axis=-1))
        outs["p_sc"].append(ntsc)
        outs["p_mk"].append(mk.reshape(bp, mem_tokens, mem_heads, mem_hd))
        outs["p_mv"].append(mv.reshape(bp, mem_tokens, mem_heads, mem_hd))
        h = _rms_matmul(hs, lp["norm_mix"], lp["w_in"]).reshape(ls, bs, -1)
        y_ssd, y_sc, m_ssm, ntx, ntb, ntc, ntsc = _mixer(
            h, state_ssm[i].reshape(bs, w, n), jnp.transpose(state_ssd_conv[i], (1, 0, 2)),
            jnp.transpose(state_short_conv[i], (1, 0, 2)), lp["mix"], time_major=True, lc=ls, nb=nb_s, gb=1)
        x1 = _matmul_res([y_ssd.reshape(ls * bs, w), y_sc.reshape(ls * bs, w)],
                         [lp["w_out_ssd"], lp["w_out_sc"]], hs)
        x1 = jnp.transpose(x1.reshape(ls, bs, d), (1, 0, 2)).reshape(bs * ls, d)
        x2, delta = _attn_peer(x1, cache_mem_k[i], cache_mem_v[i], lp, (bs, ls), mem_heads)
        if i == depth - 1:
            hs_bm = _add_rms(x2, delta, norm_final)
        else:
            hs_bm = x2 + delta
        hs = jnp.transpose(hs_bm.reshape(bs, ls, d), (1, 0, 2)).reshape(ls * bs, d) if i < depth - 1 else hs_bm
        outs["s_ssm"].append(m_ssm.reshape(bs, heads, p, n))
        outs["s_ssd"].append(jnp.transpose(jnp.concatenate([ntx, ntb, ntc], axis=-1), (1, 0, 2)))
        outs["s_sc"].append(jnp.transpose(ntsc, (1, 0, 2)))
    y_prompt = hp.reshape(bp, lp_, d)
    y_sample = hs.reshape(bs, ls, d)
    return (y_prompt, y_sample, jnp.stack(outs["p_ssm"]), jnp.stack(outs["p_ssd"]), jnp.stack(outs["p_sc"]),
            jnp.stack(outs["p_mk"]), jnp.stack(outs["p_mv"]), jnp.stack(outs["s_ssm"]), jnp.stack(outs["s_ssd"]),
            jnp.stack(outs["s_sc"]))
```
